```python
import jax, jax.numpy as jnp
from jax import lax
import numpy as np

D_MODEL = 4096
BATCH = 4
SEQ = 4096
DEPTH = 4
DEC_BATCH = 8
DEC_SEQ = 32
PAST_LEN = 2048

CHUNK = 64
HEAD_DIM = 128
GDN_WIDTH = 3 * D_MODEL // 8
GDN_HEADS = GDN_WIDTH // HEAD_DIM
SB_WIDTH = 3 * D_MODEL // 8
SB_HEADS = SB_WIDTH // HEAD_DIM
SC_WIDTH = D_MODEL - GDN_WIDTH - SB_WIDTH
D_MIX = GDN_WIDTH + SB_WIDTH + SC_WIDTH
D_IN = 4 * GDN_WIDTH + 2 * GDN_HEADS + 4 * SB_WIDTH + 4 * SC_WIDTH
GDN_CONV = 4
SC_CONV = 3
SB_BLOCK = 128
EPS = 1e-6

kernel_name = "hybrid_gdn_stickbreak_shortconv_stream_step"


def rms_norm(x, g):
    xf = x.astype(jnp.float32)
    y = xf * lax.rsqrt(jnp.mean(xf * xf, axis=-1, keepdims=True) + EPS)
    return (y * g.astype(jnp.float32)).astype(x.dtype)


def l2_normalize(x):
    return x * lax.rsqrt(jnp.sum(x * x, axis=-1, keepdims=True) + EPS)


def causal_dwconv(x, w, prefix):
    width = w.shape[0]
    t = x.shape[1]
    xp = jnp.concatenate([prefix.astype(x.dtype), x], axis=1)
    w = w.astype(x.dtype)
    y = xp[:, 0:t] * w[0]
    for i in range(1, width):
        y = y + xp[:, i:i + t] * w[i]
    return y, xp[:, t:]


def gated_delta_rule(q, k, v, g, beta, s0):
    bn, t, h, dk = q.shape
    n = t // CHUNK

    def chunks(x):
        x = x.reshape((bn, n, CHUNK) + x.shape[2:])
        return x.transpose((1, 0, 3, 2) + tuple(range(4, x.ndim)))

    qc, kc, vc, gc, bc = chunks(q), chunks(k), chunks(v), chunks(g), chunks(beta)
    gcum = jnp.cumsum(gc, axis=-1)
    idx = jnp.arange(CHUNK)
    incl = idx[:, None] >= idx[None, :]
    strict = idx[:, None] > idx[None, :]
    diff = gcum[..., :, None] - gcum[..., None, :]
    decay = jnp.where(incl, jnp.exp(jnp.where(incl, diff, 0.0)), 0.0)
    kbeta = kc * bc[..., None]
    lmat = jnp.where(strict, jnp.einsum("nbhid,nbhjd->nbhij", kbeta, kc) * decay, 0.0)
    eye = jnp.eye(CHUNK, dtype=lmat.dtype)
    tinv = lax.linalg.triangular_solve(lmat + eye, jnp.broadcast_to(eye, lmat.shape),
                                       left_side=True, lower=True, unit_diagonal=True)
    u = jnp.matmul(tinv, vc * bc[..., None])
    w = jnp.matmul(tinv, kbeta * jnp.exp(gcum)[..., None])
    attn = jnp.where(incl, jnp.einsum("nbhid,nbhjd->nbhij", qc, kc) * decay, 0.0)
    q_dec = qc * jnp.exp(gcum)[..., None]
    k_dec = kc * jnp.exp(gcum[..., -1:] - gcum)[..., None]
    g_last = jnp.exp(gcum[..., -1])

    def step(s, inp):
        u_i, w_i, a_i, qd_i, kd_i, gl_i = inp
        v_new = u_i - jnp.matmul(w_i, s)
        o = jnp.matmul(qd_i, s) + jnp.matmul(a_i, v_new)
        s = s * gl_i[..., None, None] + jnp.matmul(jnp.swapaxes(kd_i, -1, -2), v_new)
        return s, o

    s_fin, o = lax.scan(step, s0, (u, w, attn, q_dec, k_dec, g_last))
    o = o.transpose(1, 0, 3, 2, 4).reshape(bn, t, h, v.shape[-1])
    return o, s_fin


def stick_breaking(q, k, v, q_pos, k_pos):
    z = jnp.einsum("bqhd,bkhd->bhqk", q.astype(jnp.float32), k.astype(jnp.float32)) * (HEAD_DIM ** -0.5)
    causal = k_pos[None, :] < q_pos[:, None]
    log_keep = jnp.where(causal, jax.nn.log_sigmoid(-z), 0.0)
    log_between = lax.cumsum(log_keep, axis=3, reverse=True) - log_keep
    a = jnp.where(causal, jnp.exp(jax.nn.log_sigmoid(z) + log_between), 0.0)
    return jnp.einsum("bhqk,bkhd->bqhd", a, v.astype(jnp.float32))


def stick_breaking_blocked(q, k, v):
    b, t, h, d = q.shape
    nb = t // SB_BLOCK
    q_blocks = jnp.moveaxis(q.reshape(b, nb, SB_BLOCK, h, d), 1, 0)
    starts = jnp.arange(nb, dtype=jnp.int32) * SB_BLOCK
    k_pos = jnp.arange(t, dtype=jnp.int32)

    def one_block(args):
        q_blk, start = args
        return stick_breaking(q_blk, k, v, start + jnp.arange(SB_BLOCK, dtype=jnp.int32), k_pos)

    o = lax.map(one_block, (q_blocks, starts))
    return jnp.moveaxis(o, 0, 1).reshape(b, t, h, d)


def trunk_layer(x, w_in, w_out, g_pre, g_post, gdn_conv_w, gdn_a_log, gdn_dt_bias, gdn_norm, sc_conv_w,
                sb_k_past, sb_v_past, gdn_s0, gdn_conv0, sc_conv0):
    bn, t, _ = x.shape
    f32 = jnp.float32
    h = rms_norm(x, g_pre)
    proj = h @ w_in.astype(x.dtype)
    sizes = (3 * GDN_WIDTH, GDN_WIDTH, GDN_HEADS, GDN_HEADS,
             SB_WIDTH, SB_WIDTH, SB_WIDTH, SB_WIDTH,
             SC_WIDTH, SC_WIDTH, SC_WIDTH, SC_WIDTH)
    offsets = np.cumsum(sizes)[:-1].tolist()
    (a_qkv, a_z, a_a, a_b, b_q, b_k, b_v, b_z, c_b, c_c, c_h, c_z) = jnp.split(proj, offsets, axis=-1)

    qkv, gdn_conv_new = causal_dwconv(a_qkv, gdn_conv_w, gdn_conv0)
    qkv = jax.nn.silu(qkv.astype(f32))
    qa, ka, va = jnp.split(qkv, 3, axis=-1)
    qa = l2_normalize(qa.reshape(bn, t, GDN_HEADS, HEAD_DIM)) * (HEAD_DIM ** -0.5)
    ka = l2_normalize(ka.reshape(bn, t, GDN_HEADS, HEAD_DIM))
    va = va.reshape(bn, t, GDN_HEADS, HEAD_DIM)
    g = -jnp.exp(gdn_a_log.astype(f32)) * jax.nn.softplus(a_a.astype(f32) + gdn_dt_bias.astype(f32))
    beta = jax.nn.sigmoid(a_b.astype(f32))
    pad = (-t) % CHUNK

    def padt(arr):
        return jnp.pad(arr, [(0, 0), (0, pad)] + [(0, 0)] * (arr.ndim - 2))

    o_a, gdn_s_new = gated_delta_rule(padt(qa), padt(ka), padt(va), padt(g), padt(beta), gdn_s0.astype(f32))
    o_a = rms_norm(o_a[:, :t], gdn_norm).reshape(bn, t, GDN_WIDTH) * jax.nn.silu(a_z.astype(f32))

    qb = b_q.reshape(bn, t, SB_HEADS, HEAD_DIM)
    kb = b_k.reshape(bn, t, SB_HEADS, HEAD_DIM)
    vb = b_v.reshape(bn, t, SB_HEADS, HEAD_DIM)
    if sb_k_past is None:
        o_b = stick_breaking_blocked(qb, kb, vb)
    else:
        past = sb_k_past.shape[1]
        k_all = jnp.concatenate([sb_k_past.astype(x.dtype), kb], axis=1)
        v_all = jnp.concatenate([sb_v_past.astype(x.dtype), vb], axis=1)
        o_b = stick_breaking(qb, k_all, v_all, past + jnp.arange(t, dtype=jnp.int32),
                             jnp.arange(past + t, dtype=jnp.int32))
    o_b = o_b.reshape(bn, t, SB_WIDTH) * jax.nn.silu(b_z.astype(f32))

    conv_u, sc_conv_new = causal_dwconv(c_c * c_h, sc_conv_w, sc_conv0)
    o_c = c_b * conv_u * jax.nn.silu(c_z)

    mix = jnp.concatenate([o_a.astype(x.dtype), o_b.astype(x.dtype), o_c.astype(x.dtype)], axis=-1)
    x = x + rms_norm(mix @ w_out.astype(x.dtype), g_post)
    return x, (kb, vb, gdn_s_new.astype(x.dtype), gdn_conv_new, sc_conv_new)


def setup_inputs(seed: int = 0) -> dict:
    key = jax.random.key(seed)
    ks = jax.random.split(key, 18)
    nrm = jax.random.normal
    f32 = jnp.float32
    x_prompt = nrm(ks[0], (BATCH, SEQ, D_MODEL), f32)
    x_sample = nrm(ks[1], (DEC_BATCH, DEC_SEQ, D_MODEL), f32)
    cache_sb_k = nrm(ks[2], (DEPTH, DEC_BATCH, PAST_LEN, SB_HEADS, HEAD_DIM), f32)
    cache_sb_v = nrm(ks[3], (DEPTH, DEC_BATCH, PAST_LEN, SB_HEADS, HEAD_DIM), f32)
    state_gdn = 0.5 * nrm(ks[4], (DEPTH, DEC_BATCH, GDN_HEADS, HEAD_DIM, HEAD_DIM), f32)
    state_gdn_conv = nrm(ks[5], (DEPTH, DEC_BATCH, GDN_CONV - 1, 3 * GDN_WIDTH), f32)
    state_sc_conv = nrm(ks[6], (DEPTH, DEC_BATCH, SC_CONV - 1, SC_WIDTH), f32)
    w_in = nrm(ks[7], (DEPTH, D_MODEL, D_IN), f32) * (D_MODEL ** -0.5)
    w_out = nrm(ks[8], (DEPTH, D_MIX, D_MODEL), f32) * (D_MIX ** -0.5)
    norm_pre = 1.0 + 0.05 * nrm(ks[9], (DEPTH, D_MODEL), f32)
    norm_post = 1.0 + 0.05 * nrm(ks[10], (DEPTH, D_MODEL), f32)
    gdn_conv_w = nrm(ks[11], (DEPTH, GDN_CONV, 3 * GDN_WIDTH), f32) * (GDN_CONV ** -0.5)
    gdn_a_log = jnp.log(jax.random.uniform(ks[12], (DEPTH, GDN_HEADS), f32, 1.0, 16.0))
    dt = jnp.exp(jax.random.uniform(ks[13], (DEPTH, GDN_HEADS), f32, float(np.log(1e-3)), float(np.log(1e-1))))
    gdn_dt_bias = dt + jnp.log(-jnp.expm1(-dt))
    gdn_norm = 1.0 + 0.05 * nrm(ks[14], (DEPTH, HEAD_DIM), f32)
    sc_conv_w = nrm(ks[15], (DEPTH, SC_CONV, SC_WIDTH), f32) * (SC_CONV ** -0.5)
    return {"x_prompt": x_prompt, "x_sample": x_sample,
            "cache_sb_k": cache_sb_k, "cache_sb_v": cache_sb_v,
            "state_gdn": state_gdn, "state_gdn_conv": state_gdn_conv, "state_sc_conv": state_sc_conv,
            "w_in": w_in, "w_out": w_out, "norm_pre": norm_pre, "norm_post": norm_post,
            "gdn_conv_w": gdn_conv_w, "gdn_a_log": gdn_a_log, "gdn_dt_bias": gdn_dt_bias,
            "gdn_norm": gdn_norm, "sc_conv_w": sc_conv_w}


def reference(x_prompt, x_sample, cache_sb_k, cache_sb_v, state_gdn, state_gdn_conv, state_sc_conv,
              w_in, w_out, norm_pre, norm_post, gdn_conv_w, gdn_a_log, gdn_dt_bias, gdn_norm, sc_conv_w):
    bp = x_prompt.shape[0]
    dt_p = x_prompt.dtype
    s0_p = jnp.zeros((bp, GDN_HEADS, HEAD_DIM, HEAD_DIM), dt_p)
    gconv0_p = jnp.zeros((bp, GDN_CONV - 1, 3 * GDN_WIDTH), dt_p)
    sconv0_p = jnp.zeros((bp, SC_CONV - 1, SC_WIDTH), dt_p)
    yp, ys = x_prompt, x_sample
    new_p = ([], [], [], [], [])
    new_s = ([], [], [], [], [])
    for l in range(DEPTH):
        params = (w_in[l], w_out[l], norm_pre[l], norm_post[l], gdn_conv_w[l], gdn_a_log[l],
                  gdn_dt_bias[l], gdn_norm[l], sc_conv_w[l])
        yp, st_p = trunk_layer(yp, *params, None, None, s0_p, gconv0_p, sconv0_p)
        ys, st_s = trunk_layer(ys, *params, cache_sb_k[l], cache_sb_v[l], state_gdn[l],
                               state_gdn_conv[l], state_sc_conv[l])
        for i in range(5):
            new_p[i].append(st_p[i])
            new_s[i].append(st_s[i])
    sb_k_p, sb_v_p, gdn_p, gdn_conv_p, sc_conv_p = [jnp.stack(a, axis=0) for a in new_p]
    sb_k_s, sb_v_s, gdn_s, gdn_conv_s, sc_conv_s = [jnp.stack(a, axis=0) for a in new_s]
    return (yp, ys, sb_k_p, sb_v_p, gdn_p, gdn_conv_p, sc_conv_p,
            sb_k_s, sb_v_s, gdn_s, gdn_conv_s, sc_conv_s)
```

```python
import functools

import jax
import jax.numpy as jnp
from jax import lax
from jax.experimental import pallas as pl
from jax.experimental.pallas import tpu as pltpu

F32 = jnp.float32
BF16 = jnp.bfloat16

HEAD_DIM = 128
CHUNK = 64
EPS = 1e-6
GATE_LANES = 128
VMEM_LIMIT = 56 * 1024 * 1024


def _cparams(sem):
    return pltpu.CompilerParams(dimension_semantics=sem, vmem_limit_bytes=VMEM_LIMIT)


def _pick(n, prefs):
    for p in prefs:
        if n % p == 0:
            return p
    return n


def _dot(a, b):
    return jnp.dot(a, b, preferred_element_type=F32)


def _dot_nt(a, b):
    return lax.dot_general(a, b, (((1,), (1,)), ((), ())), preferred_element_type=F32)


def _dot_tn(a, b):
    return lax.dot_general(a, b, (((0,), (0,)), ((), ())), preferred_element_type=F32)


def _split2(x):
    hi = x.astype(BF16)
    lo = (x - hi.astype(F32)).astype(BF16)
    return hi, lo


def _split3(x):
    hi = x.astype(BF16)
    r = x - hi.astype(F32)
    mid = r.astype(BF16)
    lo = (r - mid.astype(F32)).astype(BF16)
    return hi, mid, lo


def _sigmoid(x):
    return 1.0 / (1.0 + jnp.exp(-x))


def _silu(x):
    return x * _sigmoid(x)


def _softplus(x):
    return jnp.maximum(x, 0.0) + jnp.log1p(jnp.exp(-jnp.abs(x)))


def _proj_kernel(x_ref, g_ref, w_ref, wg_ref, o_ref, og_ref, h_ref):
    @pl.when(pl.program_id(1) == 0)
    def _():
        x = x_ref[...]
        ms = jnp.mean(x * x, axis=-1, keepdims=True)
        h = ((x * lax.rsqrt(ms + EPS)) * g_ref[...]).astype(BF16)
        h_ref[...] = h
        og_ref[...] = _dot(h, wg_ref[...])

    o_ref[...] = _dot(h_ref[...], w_ref[...])


def _proj(x2d, g_row, w_main, w_gate):
    m, d = x2d.shape
    n = w_main.shape[1]
    tm = _pick(m, (512, 256, 128, 64, 32, 16, 8))
    tn = _pick(n, (512, 256, 128))
    return pl.pallas_call(
        _proj_kernel,
        grid=(m // tm, n // tn),
        in_specs=[
            pl.BlockSpec((tm, d), lambda i, j: (i, 0)),
            pl.BlockSpec((1, d), lambda i, j: (0, 0)),
            pl.BlockSpec((d, tn), lambda i, j: (0, j)),
            pl.BlockSpec((d, 2 * GATE_LANES), lambda i, j: (0, 0)),
        ],
        out_specs=[
            pl.BlockSpec((tm, tn), lambda i, j: (i, j)),
            pl.BlockSpec((tm, 2 * GATE_LANES), lambda i, j: (i, 0)),
        ],
        out_shape=[jax.ShapeDtypeStruct((m, n), F32),
                   jax.ShapeDtypeStruct((m, 2 * GATE_LANES), F32)],
        scratch_shapes=[pltpu.VMEM((tm, d), BF16)],
        compiler_params=_cparams(("parallel", "arbitrary")),
        name="proj",
    )(x2d, g_row, w_main, w_gate)


def _gdn_pre_kernel(x_ref, og_ref, cw_ref, pre_ref, alog_ref, dtb_ref, e_ref, linc_ref,
                    qkv_ref, bx_ref, gx_ref, xs_ref, *, tt, tt_out, nh, kw):
    i = pl.program_id(1)
    hist = kw - 1
    base = 8

    @pl.when(i == 0)
    def _():
        xs_ref[base - hist:base, :] = pre_ref[0]

    xs_ref[base:base + tt, :] = x_ref[...]
    for c in range(3 * nh):
        cs = slice(c * HEAD_DIM, (c + 1) * HEAD_DIM)
        y = xs_ref[base - hist:base - hist + tt, cs] * cw_ref[0:1, cs]
        for t in range(1, kw):
            y = y + xs_ref[base - hist + t:base - hist + t + tt, cs] * cw_ref[t:t + 1, cs]
        y = _silu(y)
        if c < 2 * nh:
            y = y * lax.rsqrt(jnp.sum(y * y, axis=-1, keepdims=True) + EPS)
        if c < nh:
            y = y * (HEAD_DIM ** -0.5)
        qkv_ref[0:tt, cs] = y
    if tt_out > tt:
        qkv_ref[tt:tt_out, :] = jnp.zeros((tt_out - tt, qkv_ref.shape[1]), F32)
    xs_ref[base - hist:base, :] = xs_ref[base + tt - hist:base + tt, :]

    og = og_ref[...]
    g = -jnp.exp(alog_ref[...]) * _softplus(og[:, :GATE_LANES] + dtb_ref[...])
    beta = _sigmoid(og[:, GATE_LANES:])
    if tt_out > tt:
        pad = jnp.zeros((tt_out - tt, GATE_LANES), F32)
        g = jnp.concatenate([g, pad], axis=0)
        beta = jnp.concatenate([beta, pad], axis=0)
    e = e_ref[...]
    linc = linc_ref[...]
    bx = None
    for p in _split3(beta):
        t = _dot(p, e)
        bx = t if bx is None else bx + t
    bx_ref[...] = bx
    for r in range(tt_out // CHUNK):
        rs = slice(r * CHUNK, (r + 1) * CHUNK)
        gc = None
        for p in _split3(g[rs]):
            t = _dot(linc, p)
            gc = t if gc is None else gc + t
        gcx = None
        for p in _split3(gc):
            t = _dot(p, e)
            gcx = t if gcx is None else gcx + t
        gx_ref[rs, :] = gcx


def _gdn_pre(proj, og, conv_w, prefix, alog_row, dtb_row, expand, linc, *, b, t, t_out, nh):
    gw = nh * HEAD_DIM
    kw = conv_w.shape[0]
    if t_out > t:
        tt, tt_out = t, t_out
    else:
        tt = _pick(t, (256, 128, 64))
        tt_out = tt
    nt = t // tt
    kern = functools.partial(_gdn_pre_kernel, tt=tt, tt_out=tt_out, nh=nh, kw=kw)
    return pl.pallas_call(
        kern,
        grid=(b, nt),
        in_specs=[
            pl.BlockSpec((tt, 3 * gw), lambda bi, i: (bi * nt + i, 0)),
            pl.BlockSpec((tt, 2 * GATE_LANES), lambda bi, i: (bi * nt + i, 0)),
            pl.BlockSpec((kw, 3 * gw), lambda bi, i: (0, 0)),
            pl.BlockSpec((1, kw - 1, 3 * gw), lambda bi, i: (bi, 0, 0)),
            pl.BlockSpec((1, GATE_LANES), lambda bi, i: (0, 0)),
            pl.BlockSpec((1, GATE_LANES), lambda bi, i: (0, 0)),
            pl.BlockSpec((GATE_LANES, gw), lambda bi, i: (0, 0)),
            pl.BlockSpec((CHUNK, CHUNK), lambda bi, i: (0, 0)),
        ],
        out_specs=[
            pl.BlockSpec((tt_out, 3 * gw), lambda bi, i: (bi * nt + i, 0)),
            pl.BlockSpec((tt_out, gw), lambda bi, i: (bi * nt + i, 0)),
            pl.BlockSpec((tt_out, gw), lambda bi, i: (bi * nt + i, 0)),
        ],
        out_shape=[jax.ShapeDtypeStruct((b * nt * tt_out, 3 * gw), F32),
                   jax.ShapeDtypeStruct((b * nt * tt_out, gw), F32),
                   jax.ShapeDtypeStruct((b * nt * tt_out, gw), F32)],
        scratch_shapes=[pltpu.VMEM((tt + 8, 3 * gw), F32)],
        compiler_params=_cparams(("parallel", "arbitrary")),
        name="gdn_pre",
    )(proj, og, conv_w, prefix, alog_row, dtb_row, expand, linc)


def _gdn_kernel(q_ref, k_ref, v_ref, bx_ref, gx_ref, az_ref, s0_ref, gn_ref,
                o_ref, sout_ref, s_ref, *, hg, cpb, t_valid):
    i = pl.program_id(2)
    c2 = 2 * CHUNK

    @pl.when(i == 0)
    def _():
        s_ref[...] = s0_ref[0]

    row = lax.broadcasted_iota(jnp.int32, (CHUNK, c2), 0)
    lane = lax.broadcasted_iota(jnp.int32, (CHUNK, c2), 1)
    col = jnp.where(lane >= CHUNK, lane - CHUNK, lane)
    incl = row >= col
    strict = row > col
    right = lane >= CHUNK
    eye_right = jnp.where(right & (row == col), 1.0, 0.0).astype(F32)
    onehot0 = jnp.where(lane == 0, 1.0, 0.0).astype(BF16)
    zrow_f = jnp.zeros((CHUNK, HEAD_DIM), F32)
    zrow_b = jnp.zeros((CHUNK, HEAD_DIM), BF16)

    for c in range(cpb):
        rs = slice(c * CHUNK, (c + 1) * CHUNK)
        nvalid = min(max(t_valid - c * CHUNK, 0), CHUNK)
        for h in range(hg):
            cs = slice(h * HEAD_DIM, (h + 1) * HEAD_DIM)
            q = q_ref[rs, cs]
            k = k_ref[rs, cs]
            v = v_ref[rs, cs]
            bx = bx_ref[rs, cs]
            gx = gx_ref[rs, cs]
            ex = jnp.exp(gx)
            gl = gx[CHUNK - 1:CHUNK, :]
            exl = jnp.exp(gl - gx)
            egl = jnp.exp(gl)
            kb = k * bx
            vb = v * bx
            kbe = kb * ex
            qd = q * ex
            kd = k * exl

            rt = None
            for p in _split3(gx):
                t = _dot_nt(onehot0, jnp.concatenate([p, p], axis=0))
                rt = t if rt is None else rt + t
            dec = jnp.where(incl, jnp.exp(jnp.where(incl, gx - rt, 0.0)), 0.0)

            kbf = k.astype(BF16)
            qk = _dot_nt(jnp.concatenate([kb, q], axis=0).astype(BF16),
                         jnp.concatenate([kbf, kbf], axis=0))
            lmat = jnp.where(strict, qk[:CHUNK] * dec, 0.0)
            attn = jnp.where(incl & (lane < CHUNK), qk[CHUNK:] * dec, 0.0)

            w = jnp.where(right, eye_right, -lmat)
            for _ in range(6):
                w_hi, w_lo = _split2(w)
                prod = (_dot(w_hi, jnp.concatenate([w_hi, zrow_b], axis=0))
                        + _dot(w_lo, jnp.concatenate([w_hi, zrow_b], axis=0))
                        + _dot(w_hi, jnp.concatenate([w_lo, zrow_b], axis=0)))
                w = prod + jnp.where(right, w, 0.0)

            rhs = jnp.concatenate([vb, kbe], axis=1).astype(BF16)
            rhs = jnp.concatenate([jnp.zeros_like(rhs), rhs], axis=0)
            tu = _dot(w.astype(BF16), rhs)
            u = tu[:, :HEAD_DIM]
            wk = tu[:, HEAD_DIM:]

            s = s_ref[h]
            ws = _dot(jnp.concatenate([wk, qd], axis=0).astype(BF16), s.astype(BF16))
            vnew = u - ws[:CHUNK]
            vnb = vnew.astype(BF16)
            o = ws[CHUNK:] + _dot(attn.astype(BF16), jnp.concatenate([vnb, zrow_b], axis=0))
            s_ref[h] = s * egl + _dot_tn(kd.astype(BF16), vnb)

            if nvalid > 0:
                ov = o[:nvalid]
                on = (ov * lax.rsqrt(jnp.mean(ov * ov, axis=-1, keepdims=True) + EPS)) * gn_ref[...]
                orow = slice(c * CHUNK, c * CHUNK + nvalid)
                o_ref[orow, cs] = (on * _silu(az_ref[orow, cs])).astype(o_ref.dtype)

    @pl.when(i == pl.num_programs(2) - 1)
    def _():
        sout_ref[0] = s_ref[...]


def _gdn(qkv, bx, gx, proj, s0, gn_row, *, b, t_pad, t_valid, nh, az_col0):
    gw = nh * HEAD_DIM
    hg = _pick(nh, (4, 3, 2, 1))
    ng = nh // hg
    if t_valid < t_pad:
        cpb = t_pad // CHUNK
    else:
        cpb = 2 if t_pad % (2 * CHUNK) == 0 else 1
    rb = cpb * CHUNK
    nblk = t_pad // rb
    rv = min(rb, t_valid)
    assert nblk == 1 or rv == rb
    hw = hg * HEAD_DIM
    azb = az_col0 // hw
    assert az_col0 % hw == 0
    kern = functools.partial(_gdn_kernel, hg=hg, cpb=cpb, t_valid=rv)
    qspec = lambda off: pl.BlockSpec((rb, hw), lambda bi, g, i, off=off: (bi * nblk + i, off + g))
    return pl.pallas_call(
        kern,
        grid=(b, ng, nblk),
        in_specs=[
            qspec(0), qspec(ng), qspec(2 * ng),
            pl.BlockSpec((rb, hw), lambda bi, g, i: (bi * nblk + i, g)),
            pl.BlockSpec((rb, hw), lambda bi, g, i: (bi * nblk + i, g)),
            pl.BlockSpec((rv, hw), lambda bi, g, i: (bi * nblk + i, azb + g)),
            pl.BlockSpec((1, hg, HEAD_DIM, HEAD_DIM), lambda bi, g, i: (bi, g, 0, 0)),
            pl.BlockSpec((1, HEAD_DIM), lambda bi, g, i: (0, 0)),
        ],
        out_specs=[
            pl.BlockSpec((rv, hw), lambda bi, g, i: (bi * nblk + i, g)),
            pl.BlockSpec((1, hg, HEAD_DIM, HEAD_DIM), lambda bi, g, i: (bi, g, 0, 0)),
        ],
        out_shape=[jax.ShapeDtypeStruct((b * nblk * rv, gw), BF16),
                   jax.ShapeDtypeStruct((b, nh, HEAD_DIM, HEAD_DIM), F32)],
        scratch_shapes=[pltpu.VMEM((hg, HEAD_DIM, HEAD_DIM), F32)],
        compiler_params=_cparams(("parallel", "parallel", "arbitrary")),
        name="gdn",
    )(qkv, qkv, qkv, bx, gx, proj, s0, gn_row)


def _sb_block(q_bf, k_blk, v_blk, tri, run, mask):
    z = _dot_nt(q_bf, k_blk.astype(BF16)) * (HEAD_DIM ** -0.5)
    ls = jnp.minimum(z, 0.0) - jnp.log1p(jnp.exp(-jnp.abs(z)))
    lk = ls - z
    if mask is not None:
        lk = jnp.where(mask, lk, 0.0)
    hi, lo = _split2(lk)
    between = _dot(hi, tri) + _dot(lo, tri)
    a = jnp.exp(ls + between + run)
    if mask is not None:
        a = jnp.where(mask, a, 0.0)
    pv = _dot(a.astype(BF16), v_blk.astype(BF16))
    return pv, jnp.sum(lk, axis=-1, keepdims=True)


def _sb_prompt_kernel(q_ref, k_ref, v_ref, z_ref, tri_ref, o_ref, acc_ref, run_ref, *, tq):
    qi = pl.program_id(2)
    q_bf = q_ref[...].astype(BF16)
    tri = tri_ref[...]
    row = lax.broadcasted_iota(jnp.int32, (tq, tq), 0)
    col = lax.broadcasted_iota(jnp.int32, (tq, tq), 1)
    d0 = pl.multiple_of(qi * tq, tq)
    pv, rsum = _sb_block(q_bf, k_ref[pl.ds(d0, tq), :], v_ref[pl.ds(d0, tq), :], tri,
                         jnp.zeros((tq, 1), F32), col < row)
    acc_ref[...] = pv
    run_ref[...] = jnp.broadcast_to(rsum, run_ref.shape)

    def body(step, carry):
        k0 = pl.multiple_of((qi - 1 - step) * tq, tq)
        run = run_ref[...][:, 0:1]
        pv, rsum = _sb_block(q_bf, k_ref[pl.ds(k0, tq), :], v_ref[pl.ds(k0, tq), :], tri, run, None)
        acc_ref[...] += pv
        run_ref[...] += jnp.broadcast_to(rsum, run_ref.shape)
        return carry

    lax.fori_loop(0, qi, body, 0)
    o_ref[...] = (acc_ref[...] * _silu(z_ref[...])).astype(o_ref.dtype)


def _sb_prompt(proj, tri, *, b, t, nh, q_col0):
    sw = nh * HEAD_DIM
    tq = tri.shape[0]
    nq = t // tq
    qb = q_col0 // HEAD_DIM
    kern = functools.partial(_sb_prompt_kernel, tq=tq)
    return pl.pallas_call(
        kern,
        grid=(b, nh, nq),
        in_specs=[
            pl.BlockSpec((tq, HEAD_DIM), lambda bi, h, i: (bi * nq + i, qb + h)),
            pl.BlockSpec((t, HEAD_DIM), lambda bi, h, i: (bi, qb + nh + h)),
            pl.BlockSpec((t, HEAD_DIM), lambda bi, h, i: (bi, qb + 2 * nh + h)),
            pl.BlockSpec((tq, HEAD_DIM), lambda bi, h, i: (bi * nq + i, qb + 3 * nh + h)),
            pl.BlockSpec((tq, tq), lambda bi, h, i: (0, 0)),
        ],
        out_specs=pl.BlockSpec((tq, HEAD_DIM), lambda bi, h, i: (bi * nq + i, h)),
        out_shape=jax.ShapeDtypeStruct((b * t, sw), BF16),
        scratch_shapes=[pltpu.VMEM((tq, HEAD_DIM), F32), pltpu.VMEM((tq, HEAD_DIM), F32)],
        compiler_params=_cparams(("parallel", "parallel", "arbitrary")),
        name="sb_prompt",
    )(proj, proj, proj, proj, tri)


def _sb_sample_kernel(q_ref, k_ref, v_ref, z_ref, kp_ref, vp_ref, trin_ref, tri_ref, o_ref,
                      acc_ref, run_ref, *, t, tk, npast):
    q_bf = q_ref[...].astype(BF16)
    row = lax.broadcasted_iota(jnp.int32, (t, t), 0)
    col = lax.broadcasted_iota(jnp.int32, (t, t), 1)
    pv, rsum = _sb_block(q_bf, k_ref[...], v_ref[...], trin_ref[...], jnp.zeros((t, 1), F32), col < row)
    acc_ref[...] = pv
    run_ref[...] = jnp.broadcast_to(rsum, run_ref.shape)
    tri = tri_ref[...]

    def body(step, carry):
        k0 = pl.multiple_of((npast - 1 - step) * tk, tk)
        run = run_ref[...][:, 0:1]
        pv, rsum = _sb_block(q_bf, kp_ref[pl.ds(k0, tk), :], vp_ref[pl.ds(k0, tk), :], tri, run, None)
        acc_ref[...] += pv
        run_ref[...] += jnp.broadcast_to(rsum, run_ref.shape)
        return carry

    lax.fori_loop(0, npast, body, 0)
    o_ref[...] = (acc_ref[...] * _silu(z_ref[...])).astype(o_ref.dtype)


def _sb_sample(proj, k_past, v_past, tri_new, tri, *, b, t, nh, q_col0):
    sw = nh * HEAD_DIM
    past = k_past.shape[0] // b
    tk = tri.shape[0]
    assert past % tk == 0
    qb = q_col0 // HEAD_DIM
    kern = functools.partial(_sb_sample_kernel, t=t, tk=tk, npast=past // tk)
    cur = lambda off: pl.BlockSpec((t, HEAD_DIM), lambda bi, h, off=off: (bi, qb + off * nh + h))
    return pl.pallas_call(
        kern,
        grid=(b, nh),
        in_specs=[
            cur(0), cur(1), cur(2), cur(3),
            pl.BlockSpec((past, HEAD_DIM), lambda bi, h: (bi, h)),
            pl.BlockSpec((past, HEAD_DIM), lambda bi, h: (bi, h)),
            pl.BlockSpec((t, t), lambda bi, h: (0, 0)),
            pl.BlockSpec((tk, tk), lambda bi, h: (0, 0)),
        ],
        out_specs=pl.BlockSpec((t, HEAD_DIM), lambda bi, h: (bi, h)),
        out_shape=jax.ShapeDtypeStruct((b * t, sw), BF16),
        scratch_shapes=[pltpu.VMEM((t, HEAD_DIM), F32), pltpu.VMEM((t, HEAD_DIM), F32)],
        compiler_params=_cparams(("parallel", "parallel")),
        name="sb_sample",
    )(proj, proj, proj, proj, k_past, v_past, tri_new, tri)


def _sc_kernel(cb_ref, cc_ref, ch_ref, cz_ref, cw_ref, pre_ref, o_ref, tail_ref, us_ref, *, tt, kw):
    i = pl.program_id(1)
    hist = kw - 1
    base = 8

    @pl.when(i == 0)
    def _():
        us_ref[base - hist:base, :] = pre_ref[0]

    us_ref[base:base + tt, :] = cc_ref[...] * ch_ref[...]
    y = us_ref[base - hist:base - hist + tt, :] * cw_ref[0:1, :]
    for t in range(1, kw):
        y = y + us_ref[base - hist + t:base - hist + t + tt, :] * cw_ref[t:t + 1, :]
    o_ref[...] = ((cb_ref[...] * y) * _silu(cz_ref[...])).astype(o_ref.dtype)
    tail = us_ref[base + tt - hist:base + tt, :]
    us_ref[base - hist:base, :] = tail

    @pl.when(i == pl.num_programs(1) - 1)
    def _():
        tail_ref[0] = tail


def _sc(proj, conv_w, prefix, *, b, t, cw, c_col0):
    kw = conv_w.shape[0]
    tt = _pick(t, (256, 128, 64, 32, 16, 8))
    nt = t // tt
    cb = c_col0 // cw
    assert c_col0 % cw == 0
    kern = functools.partial(_sc_kernel, tt=tt, kw=kw)
    col = lambda off: pl.BlockSpec((tt, cw), lambda bi, i, off=off: (bi * nt + i, cb + off))
    return pl.pallas_call(
        kern,
        grid=(b, nt),
        in_specs=[
            col(0), col(1), col(2), col(3),
            pl.BlockSpec((kw, cw), lambda bi, i: (0, 0)),
            pl.BlockSpec((1, kw - 1, cw), lambda bi, i: (bi, 0, 0)),
        ],
        out_specs=[
            pl.BlockSpec((tt, cw), lambda bi, i: (bi * nt + i, 0)),
            pl.BlockSpec((1, kw - 1, cw), lambda bi, i: (bi, 0, 0)),
        ],
        out_shape=[jax.ShapeDtypeStruct((b * t, cw), BF16),
                   jax.ShapeDtypeStruct((b, kw - 1, cw), F32)],
        scratch_shapes=[pltpu.VMEM((tt + 8, cw), F32)],
        compiler_params=_cparams(("parallel", "arbitrary")),
        name="sc",
    )(proj, proj, proj, proj, conv_w, prefix)


def _outproj_kernel(a_ref, b_ref, c_ref, wa_ref, wb_ref, wc_ref, o_ref):
    o_ref[...] = (_dot(a_ref[...], wa_ref[...]) + _dot(b_ref[...], wb_ref[...])
                  + _dot(c_ref[...], wc_ref[...]))


def _outproj(oa, ob, oc, wa, wb, wc):
    m = oa.shape[0]
    d = wa.shape[1]
    tm = _pick(m, (1024, 512, 256, 128, 64, 32, 16))
    tn = _pick(d, (512, 256, 128))
    return pl.pallas_call(
        _outproj_kernel,
        grid=(m // tm, d // tn),
        in_specs=[
            pl.BlockSpec((tm, oa.shape[1]), lambda i, j: (i, 0)),
            pl.BlockSpec((tm, ob.shape[1]), lambda i, j: (i, 0)),
            pl.BlockSpec((tm, oc.shape[1]), lambda i, j: (i, 0)),
            pl.BlockSpec((wa.shape[0], tn), lambda i, j: (0, j)),
            pl.BlockSpec((wb.shape[0], tn), lambda i, j: (0, j)),
            pl.BlockSpec((wc.shape[0], tn), lambda i, j: (0, j)),
        ],
        out_specs=pl.BlockSpec((tm, tn), lambda i, j: (i, j)),
        out_shape=jax.ShapeDtypeStruct((m, d), F32),
        compiler_params=_cparams(("parallel", "parallel")),
        name="outproj",
    )(oa, ob, oc, wa, wb, wc)


def _post_kernel(x_ref, y_ref, g_ref, o_ref):
    y = y_ref[...]
    ms = jnp.mean(y * y, axis=-1, keepdims=True)
    o_ref[...] = x_ref[...] + (y * lax.rsqrt(ms + EPS)) * g_ref[...]


def _post(x2d, y2d, g_row):
    m, d = x2d.shape
    tm = _pick(m, (256, 128, 64, 32, 16, 8))
    spec = pl.BlockSpec((tm, d), lambda i: (i, 0))
    return pl.pallas_call(
        _post_kernel,
        grid=(m // tm,),
        in_specs=[spec, spec, pl.BlockSpec((1, d), lambda i: (0, 0))],
        out_specs=spec,
        out_shape=jax.ShapeDtypeStruct((m, d), F32),
        compiler_params=_cparams(("parallel",)),
        name="post",
    )(x2d, y2d, g_row)


def _layer(x, lw, consts, *, nh_g, nh_s, cw, sb_past, gdn_s0, gdn_conv0, sc_conv0):
    b, t, d = x.shape
    gw, sw = nh_g * HEAD_DIM, nh_s * HEAD_DIM
    x2d = x.reshape(b * t, d)
    proj, og = _proj(x2d, lw["g_pre"], lw["w_main"], lw["w_gate"])

    t_pad = -(-t // CHUNK) * CHUNK
    qkv, bx, gx = _gdn_pre(proj, og, lw["gdn_conv_w"], gdn_conv0, lw["alog"], lw["dtb"],
                           consts["expand"], consts["linc"], b=b, t=t, t_out=t_pad, nh=nh_g)
    o_a, s_new = _gdn(qkv, bx, gx, proj, gdn_s0, lw["gn"], b=b, t_pad=t_pad, t_valid=t,
                      nh=nh_g, az_col0=3 * gw)

    q_col0 = 4 * gw
    if sb_past is None:
        o_b = _sb_prompt(proj, consts["tri"], b=b, t=t, nh=nh_s, q_col0=q_col0)
    else:
        o_b = _sb_sample(proj, sb_past[0], sb_past[1], consts["tri_new"], consts["tri"],
                         b=b, t=t, nh=nh_s, q_col0=q_col0)

    c_col0 = 4 * gw + 4 * sw
    o_c, sc_tail = _sc(proj, lw["sc_conv_w"], sc_conv0, b=b, t=t, cw=cw, c_col0=c_col0)

    y = _outproj(o_a, o_b, o_c, lw["w_out_a"], lw["w_out_b"], lw["w_out_c"])
    x_new = _post(x2d, y, lw["g_post"]).reshape(b, t, d)

    kcol = q_col0 + sw
    kb = proj[:, kcol:kcol + sw].reshape(b, t, nh_s, HEAD_DIM)
    vb = proj[:, kcol + sw:kcol + 2 * sw].reshape(b, t, nh_s, HEAD_DIM)
    kw = lw["gdn_conv_w"].shape[0]
    conv_new = proj[:, :3 * gw].reshape(b, t, 3 * gw)[:, t - (kw - 1):]
    return x_new, (kb, vb, s_new, conv_new, sc_tail)


def _tri(n):
    r = jnp.arange(n)
    return (r[:, None] > r[None, :]).astype(BF16)


def kernel(x_prompt, x_sample, cache_sb_k, cache_sb_v, state_gdn, state_gdn_conv, state_sc_conv,
           w_in, w_out, norm_pre, norm_post, gdn_conv_w, gdn_a_log, gdn_dt_bias, gdn_norm, sc_conv_w):
    depth, d, _ = w_in.shape
    nh_g = gdn_a_log.shape[1]
    nh_s = cache_sb_k.shape[3]
    cw = sc_conv_w.shape[2]
    gw, sw = nh_g * HEAD_DIM, nh_s * HEAD_DIM
    bp, tp, _ = x_prompt.shape
    bs, ts, _ = x_sample.shape
    past = cache_sb_k.shape[2]
    kw_g = gdn_conv_w.shape[1]
    kw_s = sc_conv_w.shape[1]
    assert nh_g <= GATE_LANES and tp >= kw_g and ts >= kw_g and tp % CHUNK == 0

    g0 = 4 * gw
    w_main = jnp.concatenate([w_in[:, :, :g0], w_in[:, :, g0 + 2 * nh_g:]], axis=2).astype(BF16)
    zpad = jnp.zeros((depth, d, GATE_LANES - nh_g), w_in.dtype)
    w_gate = jnp.concatenate([w_in[:, :, g0:g0 + nh_g], zpad,
                              w_in[:, :, g0 + nh_g:g0 + 2 * nh_g], zpad], axis=2).astype(BF16)
    w_out_b16 = w_out.astype(BF16)
    lane_pad = lambda a: jnp.pad(a.astype(F32), ((0, 0), (0, GATE_LANES - nh_g)))[:, None, :]
    alog = lane_pad(gdn_a_log)
    dtb = lane_pad(gdn_dt_bias)

    lane_head = jnp.arange(gw) // HEAD_DIM
    tq = _pick(tp, (256, 128, 64, 32, 16, 8))
    tk = _pick(past, (256, 128, 64, 32, 16, 8))
    r = jnp.arange(CHUNK)
    consts = {
        "expand": (jnp.arange(GATE_LANES)[:, None] == lane_head[None, :]).astype(BF16),
        "linc": (r[:, None] >= r[None, :]).astype(BF16),
        "tri": _tri(tq),
    }
    consts_s = dict(consts, tri=_tri(tk), tri_new=_tri(ts))

    s0_p = jnp.zeros((bp, nh_g, HEAD_DIM, HEAD_DIM), F32)
    gconv0_p = jnp.zeros((bp, kw_g - 1, 3 * gw), F32)
    sconv0_p = jnp.zeros((bp, kw_s - 1, cw), F32)

    yp, ys = x_prompt, x_sample
    new_p = ([], [], [], [], [])
    new_s = ([], [], [], [], [])
    for l in range(depth):
        lw = {
            "g_pre": norm_pre[l][None, :], "g_post": norm_post[l][None, :],
            "w_main": w_main[l], "w_gate": w_gate[l],
            "w_out_a": w_out_b16[l, :gw], "w_out_b": w_out_b16[l, gw:gw + sw], "w_out_c": w_out_b16[l, gw + sw:],
            "gdn_conv_w": gdn_conv_w[l], "alog": alog[l], "dtb": dtb[l], "gn": gdn_norm[l][None, :],
            "sc_conv_w": sc_conv_w[l],
        }
        yp, st_p = _layer(yp, lw, consts, nh_g=nh_g, nh_s=nh_s, cw=cw, sb_past=None,
                          gdn_s0=s0_p, gdn_conv0=gconv0_p, sc_conv0=sconv0_p)
        kpast = cache_sb_k[l].reshape(bs * past, sw)
        vpast = cache_sb_v[l].reshape(bs * past, sw)
        ys, st_s = _layer(ys, lw, consts_s, nh_g=nh_g, nh_s=nh_s, cw=cw, sb_past=(kpast, vpast),
                          gdn_s0=state_gdn[l], gdn_conv0=state_gdn_conv[l], sc_conv0=state_sc_conv[l])
        for i in range(5):
            new_p[i].append(st_p[i])
            new_s[i].append(st_s[i])
    outs_p = [jnp.stack(a, axis=0) for a in new_p]
    outs_s = [jnp.stack(a, axis=0) for a in new_s]
    return (yp, ys, *outs_p, *outs_s)
```

```python
import functools

import jax
import jax.numpy as jnp
from jax import lax
from jax.experimental import pallas as pl
from jax.experimental.pallas import tpu as pltpu

F32 = jnp.float32
BF16 = jnp.bfloat16

HEAD_DIM = 128
CHUNK = 64
EPS = 1e-6
GATE_LANES = 128
VMEM_LIMIT = 56 * 1024 * 1024
GDN_HEADS_PER_STEP = (12, 6, 4, 3, 2, 1)
GDN_CHUNKS_PER_STEP = (2, 1)
SB_HEADS_PER_STEP = (4, 3, 2, 1)


def _cparams(sem):
    return pltpu.CompilerParams(dimension_semantics=sem, vmem_limit_bytes=VMEM_LIMIT)


def _pick(n, prefs):
    for p in prefs:
        if n % p == 0:
            return p
    return n


def _dot(a, b):
    return jnp.dot(a, b, preferred_element_type=F32)


def _dot_nt(a, b):
    return lax.dot_general(a, b, (((1,), (1,)), ((), ())), preferred_element_type=F32)


def _dot_tn(a, b):
    return lax.dot_general(a, b, (((0,), (0,)), ((), ())), preferred_element_type=F32)


def _split2(x):
    hi = x.astype(BF16)
    lo = (x - hi.astype(F32)).astype(BF16)
    return hi, lo


def _split3(x):
    hi = x.astype(BF16)
    r = x - hi.astype(F32)
    mid = r.astype(BF16)
    lo = (r - mid.astype(F32)).astype(BF16)
    return hi, mid, lo


def _sigmoid(x):
    return 1.0 / (1.0 + jnp.exp(-x))


def _silu(x):
    return x * _sigmoid(x)


def _softplus(x):
    return jnp.maximum(x, 0.0) + jnp.log1p(jnp.exp(-jnp.abs(x)))


def _proj_kernel(x_ref, g_ref, w_ref, wg_ref, o_ref, og_ref, h_ref):
    @pl.when(pl.program_id(1) == 0)
    def _():
        x = x_ref[...]
        ms = jnp.mean(x * x, axis=-1, keepdims=True)
        h = ((x * lax.rsqrt(ms + EPS)) * g_ref[...]).astype(BF16)
        h_ref[...] = h
        og_ref[...] = _dot(h, wg_ref[...])

    o_ref[...] = _dot(h_ref[...], w_ref[...])


def _proj(x2d, g_row, w_main, w_gate):
    m, d = x2d.shape
    n = w_main.shape[1]
    tm = _pick(m, (512, 256, 128, 64, 32, 16, 8))
    tn = _pick(n, (512, 256, 128))
    return pl.pallas_call(
        _proj_kernel,
        grid=(m // tm, n // tn),
        in_specs=[
            pl.BlockSpec((tm, d), lambda i, j: (i, 0)),
            pl.BlockSpec((1, d), lambda i, j: (0, 0)),
            pl.BlockSpec((d, tn), lambda i, j: (0, j)),
            pl.BlockSpec((d, 2 * GATE_LANES), lambda i, j: (0, 0)),
        ],
        out_specs=[
            pl.BlockSpec((tm, tn), lambda i, j: (i, j)),
            pl.BlockSpec((tm, 2 * GATE_LANES), lambda i, j: (i, 0)),
        ],
        out_shape=[jax.ShapeDtypeStruct((m, n), F32),
                   jax.ShapeDtypeStruct((m, 2 * GATE_LANES), F32)],
        scratch_shapes=[pltpu.VMEM((tm, d), BF16)],
        compiler_params=_cparams(("parallel", "arbitrary")),
        name="proj",
    )(x2d, g_row, w_main, w_gate)


def _gdn_pre_kernel(x_ref, og_ref, cw_ref, pre_ref, alog_ref, dtb_ref, e_ref, linc_ref,
                    qkv_ref, bx_ref, gx_ref, xs_ref, *, tt, tt_out, nh, kw):
    i = pl.program_id(1)
    hist = kw - 1
    base = 8

    @pl.when(i == 0)
    def _():
        xs_ref[base - hist:base, :] = pre_ref[0]

    xs_ref[base:base + tt, :] = x_ref[...]
    for c in range(3 * nh):
        cs = slice(c * HEAD_DIM, (c + 1) * HEAD_DIM)
        y = xs_ref[base - hist:base - hist + tt, cs] * cw_ref[0:1, cs]
        for t in range(1, kw):
            y = y + xs_ref[base - hist + t:base - hist + t + tt, cs] * cw_ref[t:t + 1, cs]
        y = _silu(y)
        if c < 2 * nh:
            y = y * lax.rsqrt(jnp.sum(y * y, axis=-1, keepdims=True) + EPS)
        if c < nh:
            y = y * (HEAD_DIM ** -0.5)
        qkv_ref[0:tt, cs] = y
    if tt_out > tt:
        qkv_ref[tt:tt_out, :] = jnp.zeros((tt_out - tt, qkv_ref.shape[1]), F32)
    xs_ref[base - hist:base, :] = xs_ref[base + tt - hist:base + tt, :]

    og = og_ref[...]
    g = -jnp.exp(alog_ref[...]) * _softplus(og[:, :GATE_LANES] + dtb_ref[...])
    beta = _sigmoid(og[:, GATE_LANES:])
    if tt_out > tt:
        pad = jnp.zeros((tt_out - tt, GATE_LANES), F32)
        g = jnp.concatenate([g, pad], axis=0)
        beta = jnp.concatenate([beta, pad], axis=0)
    e = e_ref[...]
    linc = linc_ref[...]
    bx = None
    for p in _split3(beta):
        t = _dot(p, e)
        bx = t if bx is None else bx + t
    bx_ref[...] = bx
    for r in range(tt_out // CHUNK):
        rs = slice(r * CHUNK, (r + 1) * CHUNK)
        gc = None
        for p in _split3(g[rs]):
            t = _dot(linc, p)
            gc = t if gc is None else gc + t
        gcx = None
        for p in _split3(gc):
            t = _dot(p, e)
            gcx = t if gcx is None else gcx + t
        gx_ref[rs, :] = gcx


def _gdn_pre(proj, og, conv_w, prefix, alog_row, dtb_row, expand, linc, *, b, t, t_out, nh):
    gw = nh * HEAD_DIM
    kw = conv_w.shape[0]
    if t_out > t:
        tt, tt_out = t, t_out
    else:
        tt = _pick(t, (256, 128, 64))
        tt_out = tt
    nt = t // tt
    kern = functools.partial(_gdn_pre_kernel, tt=tt, tt_out=tt_out, nh=nh, kw=kw)
    return pl.pallas_call(
        kern,
        grid=(b, nt),
        in_specs=[
            pl.BlockSpec((tt, 3 * gw), lambda bi, i: (bi * nt + i, 0)),
            pl.BlockSpec((tt, 2 * GATE_LANES), lambda bi, i: (bi * nt + i, 0)),
            pl.BlockSpec((kw, 3 * gw), lambda bi, i: (0, 0)),
            pl.BlockSpec((1, kw - 1, 3 * gw), lambda bi, i: (bi, 0, 0)),
            pl.BlockSpec((1, GATE_LANES), lambda bi, i: (0, 0)),
            pl.BlockSpec((1, GATE_LANES), lambda bi, i: (0, 0)),
            pl.BlockSpec((GATE_LANES, gw), lambda bi, i: (0, 0)),
            pl.BlockSpec((CHUNK, CHUNK), lambda bi, i: (0, 0)),
        ],
        out_specs=[
            pl.BlockSpec((tt_out, 3 * gw), lambda bi, i: (bi * nt + i, 0)),
            pl.BlockSpec((tt_out, gw), lambda bi, i: (bi * nt + i, 0)),
            pl.BlockSpec((tt_out, gw), lambda bi, i: (bi * nt + i, 0)),
        ],
        out_shape=[jax.ShapeDtypeStruct((b * nt * tt_out, 3 * gw), F32),
                   jax.ShapeDtypeStruct((b * nt * tt_out, gw), F32),
                   jax.ShapeDtypeStruct((b * nt * tt_out, gw), F32)],
        scratch_shapes=[pltpu.VMEM((tt + 8, 3 * gw), F32)],
        compiler_params=_cparams(("parallel", "arbitrary")),
        name="gdn_pre",
    )(proj, og, conv_w, prefix, alog_row, dtb_row, expand, linc)


def _gdn_kernel(q_ref, k_ref, v_ref, bx_ref, gx_ref, az_ref, s0_ref, gn_ref,
                o_ref, sout_ref, s_ref, *, hg, cpb, t_valid):
    i = pl.program_id(2)
    c2 = 2 * CHUNK

    @pl.when(i == 0)
    def _():
        s_ref[...] = s0_ref[0]

    row = lax.broadcasted_iota(jnp.int32, (CHUNK, c2), 0)
    lane = lax.broadcasted_iota(jnp.int32, (CHUNK, c2), 1)
    col = jnp.where(lane >= CHUNK, lane - CHUNK, lane)
    incl = row >= col
    strict = row > col
    right = lane >= CHUNK
    eye_right = jnp.where(right & (row == col), 1.0, 0.0).astype(F32)
    onehot0 = jnp.where(lane == 0, 1.0, 0.0).astype(BF16)
    zrow_b = jnp.zeros((CHUNK, HEAD_DIM), BF16)

    units = [(c, h) for c in range(cpb) for h in range(hg)]
    rsl = lambda c: slice(c * CHUNK, (c + 1) * CHUNK)
    csl = lambda h: slice(h * HEAD_DIM, (h + 1) * HEAD_DIM)
    st = {}
    for u in units:
        c, h = u
        q = q_ref[rsl(c), csl(h)]
        k = k_ref[rsl(c), csl(h)]
        v = v_ref[rsl(c), csl(h)]
        bx = bx_ref[rsl(c), csl(h)]
        gx = gx_ref[rsl(c), csl(h)]
        ex = jnp.exp(gx)
        gl = gx[CHUNK - 1:CHUNK, :]
        kb = k * bx
        st[u] = dict(gx=gx, k=k, kb=kb, q=q, egl=jnp.exp(gl), kd=(k * jnp.exp(gl - gx)).astype(BF16),
                     qd=q * ex, rhs=jnp.concatenate([v * bx, kb * ex], axis=1).astype(BF16))

    for u in units:
        rt = None
        for p in _split2(st[u]["gx"]):
            t = _dot_nt(onehot0, jnp.concatenate([p, p], axis=0))
            rt = t if rt is None else rt + t
        st[u]["rt"] = rt
    for u in units:
        s = st[u]
        kbf = s["k"].astype(BF16)
        s["qk"] = _dot_nt(jnp.concatenate([s["kb"], s["q"]], axis=0).astype(BF16),
                          jnp.concatenate([kbf, kbf], axis=0))
    for u in units:
        s = st[u]
        dec = jnp.where(incl, jnp.exp(jnp.where(incl, s["gx"] - s["rt"], 0.0)), 0.0)
        lmat = jnp.where(strict, s["qk"][:CHUNK] * dec, 0.0)
        s["attn"] = jnp.where(incl & (lane < CHUNK), s["qk"][CHUNK:] * dec, 0.0).astype(BF16)
        s["w"] = jnp.where(right, eye_right, -lmat)
    for _ in range(6):
        for u in units:
            s = st[u]
            wb = s["w"].astype(BF16)
            s["prod"] = _dot(wb, jnp.concatenate([wb, zrow_b], axis=0))
        for u in units:
            s = st[u]
            s["w"] = s["prod"] + jnp.where(right, s["w"], 0.0)
    for u in units:
        s = st[u]
        rhs = jnp.concatenate([jnp.zeros_like(s["rhs"]), s["rhs"]], axis=0)
        s["tu"] = _dot(s["w"].astype(BF16), rhs)

    state = [s_ref[h] for h in range(hg)]
    for c in range(cpb):
        nvalid = min(max(t_valid - c * CHUNK, 0), CHUNK)
        ws = []
        for h in range(hg):
            s = st[(c, h)]
            lhs = jnp.concatenate([s["tu"][:, HEAD_DIM:], s["qd"]], axis=0).astype(BF16)
            ws.append(_dot(lhs, state[h].astype(BF16)))
        vnb = [(st[(c, h)]["tu"][:, :HEAD_DIM] - ws[h][:CHUNK]).astype(BF16) for h in range(hg)]
        outs = []
        for h in range(hg):
            s = st[(c, h)]
            outs.append(ws[h][CHUNK:] + _dot(s["attn"], jnp.concatenate([vnb[h], zrow_b], axis=0)))
            state[h] = state[h] * s["egl"] + _dot_tn(s["kd"], vnb[h])
        if nvalid > 0:
            for h in range(hg):
                ov = outs[h][:nvalid]
                on = (ov * lax.rsqrt(jnp.mean(ov * ov, axis=-1, keepdims=True) + EPS)) * gn_ref[...]
                orow = slice(c * CHUNK, c * CHUNK + nvalid)
                o_ref[orow, csl(h)] = (on * _silu(az_ref[orow, csl(h)])).astype(o_ref.dtype)
    for h in range(hg):
        s_ref[h] = state[h]

    @pl.when(i == pl.num_programs(2) - 1)
    def _():
        sout_ref[0] = s_ref[...]


def _gdn(qkv, bx, gx, proj, s0, gn_row, *, b, t_pad, t_valid, nh, az_col0):
    gw = nh * HEAD_DIM
    hg = _pick(nh, GDN_HEADS_PER_STEP)
    ng = nh // hg
    if t_valid < t_pad:
        cpb = t_pad // CHUNK
    else:
        cpb = _pick(t_pad // CHUNK, GDN_CHUNKS_PER_STEP)
    rb = cpb * CHUNK
    nblk = t_pad // rb
    rv = min(rb, t_valid)
    assert nblk == 1 or rv == rb
    hw = hg * HEAD_DIM
    azb = az_col0 // hw
    assert az_col0 % hw == 0
    kern = functools.partial(_gdn_kernel, hg=hg, cpb=cpb, t_valid=rv)
    qspec = lambda off: pl.BlockSpec((rb, hw), lambda bi, g, i, off=off: (bi * nblk + i, off + g))
    return pl.pallas_call(
        kern,
        grid=(b, ng, nblk),
        in_specs=[
            qspec(0), qspec(ng), qspec(2 * ng),
            pl.BlockSpec((rb, hw), lambda bi, g, i: (bi * nblk + i, g)),
            pl.BlockSpec((rb, hw), lambda bi, g, i: (bi * nblk + i, g)),
            pl.BlockSpec((rv, hw), lambda bi, g, i: (bi * nblk + i, azb + g)),
            pl.BlockSpec((1, hg, HEAD_DIM, HEAD_DIM), lambda bi, g, i: (bi, g, 0, 0)),
            pl.BlockSpec((1, HEAD_DIM), lambda bi, g, i: (0, 0)),
        ],
        out_specs=[
            pl.BlockSpec((rv, hw), lambda bi, g, i: (bi * nblk + i, g)),
            pl.BlockSpec((1, hg, HEAD_DIM, HEAD_DIM), lambda bi, g, i: (bi, g, 0, 0)),
        ],
        out_shape=[jax.ShapeDtypeStruct((b * nblk * rv, gw), BF16),
                   jax.ShapeDtypeStruct((b, nh, HEAD_DIM, HEAD_DIM), F32)],
        scratch_shapes=[pltpu.VMEM((hg, HEAD_DIM, HEAD_DIM), F32)],
        compiler_params=_cparams(("parallel", "parallel", "arbitrary")),
        name="gdn",
    )(qkv, qkv, qkv, bx, gx, proj, s0, gn_row)


SB_Z2_SCALE = (HEAD_DIM ** -0.5) * 1.4426950408889634

def _neg_abs(x):
    bits = lax.bitcast_convert_type(x, jnp.uint32) | jnp.uint32(0x80000000)
    return lax.bitcast_convert_type(bits, F32)


def _lane_tile(x, n):
    if n <= HEAD_DIM:
        return x[:, :n]
    return jnp.concatenate([x] * (n // HEAD_DIM), axis=1)


def _sb_blocks(q_bfs, k_blks, v_blks, tri, drops, mask):
    z2s = [_dot_nt(q, k.astype(BF16)) for q, k in zip(q_bfs, k_blks)]
    lss, nlks = [], []
    for z2 in z2s:
        ls = jnp.minimum(z2, 0.0) - jnp.log2(1.0 + jnp.exp2(_neg_abs(z2)))
        lss.append(ls)
        nlk = z2 - ls
        if mask is not None:
            nlk = jnp.where(mask, nlk, 0.0)
        nlks.append(nlk)
    betw = [_dot(nlk.astype(BF16), tri) for nlk in nlks]
    pvs = []
    for ls, bt, drop, v in zip(lss, betw, drops, v_blks):
        a = jnp.exp2(ls - bt - _lane_tile(drop, bt.shape[1]))
        if mask is not None:
            a = jnp.where(mask, a, 0.0)
        pvs.append(_dot(a.astype(BF16), v.astype(BF16)))
    return pvs, [jnp.sum(nlk, axis=-1, keepdims=True) for nlk in nlks]


def _sb_sweep(q_ref, z_ref, o_ref, acc_ref, drop_ref, first, blocks, nblocks, tri, *, hb):
    hsl = lambda h: slice(h * HEAD_DIM, (h + 1) * HEAD_DIM)
    q_bfs = [(q_ref[:, hsl(h)] * SB_Z2_SCALE).astype(BF16) for h in range(hb)]
    rows = q_ref.shape[0]
    k0, v0, tri0, mask0 = first
    pvs, sums = _sb_blocks(q_bfs, k0, v0, tri0, [jnp.zeros((rows, HEAD_DIM), F32)] * hb, mask0)
    for h in range(hb):
        acc_ref[:, hsl(h)] = pvs[h]
        drop_ref[:, hsl(h)] = jnp.broadcast_to(sums[h], (rows, HEAD_DIM))

    def body(step, carry):
        ks, vs = blocks(step)
        drops = [drop_ref[:, hsl(h)] for h in range(hb)]
        pvs, sums = _sb_blocks(q_bfs, ks, vs, tri, drops, None)
        for h in range(hb):
            acc_ref[:, hsl(h)] += pvs[h]
            drop_ref[:, hsl(h)] += jnp.broadcast_to(sums[h], (rows, HEAD_DIM))
        return carry

    lax.fori_loop(0, nblocks, body, 0)
    o_ref[...] = (acc_ref[...] * _silu(z_ref[...])).astype(o_ref.dtype)


def _sb_prompt_kernel(q_ref, k_ref, v_ref, z_ref, tri_ref, o_ref, acc_ref, drop_ref, *, tq, hb):
    qi = pl.program_id(2)
    hsl = lambda h: slice(h * HEAD_DIM, (h + 1) * HEAD_DIM)
    tri = tri_ref[...]
    row = lax.broadcasted_iota(jnp.int32, (tq, tq), 0)
    col = lax.broadcasted_iota(jnp.int32, (tq, tq), 1)

    def load(r0):
        return ([k_ref[pl.ds(r0, tq), hsl(h)] for h in range(hb)],
                [v_ref[pl.ds(r0, tq), hsl(h)] for h in range(hb)])

    k0, v0 = load(pl.multiple_of(qi * tq, tq))
    blocks = lambda step: load(pl.multiple_of((qi - 1 - step) * tq, tq))
    _sb_sweep(q_ref, z_ref, o_ref, acc_ref, drop_ref, (k0, v0, tri, col < row), blocks, qi, tri, hb=hb)


def _sb_heads_per_step(nh, q_col0):
    for hb in SB_HEADS_PER_STEP:
        if nh % hb == 0 and (q_col0 // HEAD_DIM) % hb == 0:
            return hb
    return 1


def _sb_prompt(proj, tri, *, b, t, nh, q_col0):
    sw = nh * HEAD_DIM
    tq = tri.shape[0]
    nq = t // tq
    hb = _sb_heads_per_step(nh, q_col0)
    hw = hb * HEAD_DIM
    ng = nh // hb
    qb = q_col0 // hw
    kern = functools.partial(_sb_prompt_kernel, tq=tq, hb=hb)

    def rows(off):
        return pl.BlockSpec((tq, hw), lambda bi, g, i: (bi * nq + i, qb + off * ng + g))

    def seq(off):
        return pl.BlockSpec((t, hw), lambda bi, g, i: (bi, qb + off * ng + g), pipeline_mode=pl.Buffered(1))

    return pl.pallas_call(
        kern,
        grid=(b, ng, nq),
        in_specs=[rows(0), seq(1), seq(2), rows(3), pl.BlockSpec((tq, tq), lambda bi, g, i: (0, 0))],
        out_specs=pl.BlockSpec((tq, hw), lambda bi, g, i: (bi * nq + i, g)),
        out_shape=jax.ShapeDtypeStruct((b * t, sw), BF16),
        scratch_shapes=[pltpu.VMEM((tq, hw), F32), pltpu.VMEM((tq, hw), F32)],
        compiler_params=_cparams(("parallel", "parallel", "arbitrary")),
        name="sb_prompt",
    )(proj, proj, proj, proj, tri)


def _sb_sample_kernel(q_ref, k_ref, v_ref, z_ref, kp_ref, vp_ref, trin_ref, tri_ref, o_ref,
                      acc_ref, drop_ref, *, t, tk, npast, hb):
    hsl = lambda h: slice(h * HEAD_DIM, (h + 1) * HEAD_DIM)
    row = lax.broadcasted_iota(jnp.int32, (t, t), 0)
    col = lax.broadcasted_iota(jnp.int32, (t, t), 1)
    k0 = [k_ref[:, hsl(h)] for h in range(hb)]
    v0 = [v_ref[:, hsl(h)] for h in range(hb)]

    def blocks(step):
        r0 = pl.multiple_of((npast - 1 - step) * tk, tk)
        return ([kp_ref[pl.ds(r0, tk), hsl(h)] for h in range(hb)],
                [vp_ref[pl.ds(r0, tk), hsl(h)] for h in range(hb)])

    _sb_sweep(q_ref, z_ref, o_ref, acc_ref, drop_ref, (k0, v0, trin_ref[...], col < row), blocks, npast,
              tri_ref[...], hb=hb)


def _sb_sample(proj, k_past, v_past, layer, tri_new, tri, *, b, t, nh, q_col0):
    sw = nh * HEAD_DIM
    past = k_past.shape[1] // b
    tk = tri.shape[0]
    assert past % tk == 0
    hb = _sb_heads_per_step(nh, q_col0)
    hw = hb * HEAD_DIM
    ng = nh // hb
    qb = q_col0 // hw
    kern = functools.partial(_sb_sample_kernel, t=t, tk=tk, npast=past // tk, hb=hb)

    def cur(off):
        return pl.BlockSpec((t, hw), lambda bi, g: (bi, qb + off * ng + g))

    old = pl.BlockSpec((None, past, hw), lambda bi, g: (layer, bi, g))
    return pl.pallas_call(
        kern,
        grid=(b, ng),
        in_specs=[
            cur(0), cur(1), cur(2), cur(3), old, old,
            pl.BlockSpec((t, t), lambda bi, g: (0, 0)),
            pl.BlockSpec((tk, tk), lambda bi, g: (0, 0)),
        ],
        out_specs=pl.BlockSpec((t, hw), lambda bi, g: (bi, g)),
        out_shape=jax.ShapeDtypeStruct((b * t, sw), BF16),
        scratch_shapes=[pltpu.VMEM((t, hw), F32), pltpu.VMEM((t, hw), F32)],
        compiler_params=_cparams(("parallel", "parallel")),
        name="sb_sample",
    )(proj, proj, proj, proj, k_past, v_past, tri_new, tri)


def _sc_kernel(cb_ref, cc_ref, ch_ref, cz_ref, cw_ref, pre_ref, o_ref, tail_ref, us_ref, *, tt, kw):
    i = pl.program_id(1)
    hist = kw - 1
    base = 8

    @pl.when(i == 0)
    def _():
        us_ref[base - hist:base, :] = pre_ref[0]

    us_ref[base:base + tt, :] = cc_ref[...] * ch_ref[...]
    y = us_ref[base - hist:base - hist + tt, :] * cw_ref[0:1, :]
    for t in range(1, kw):
        y = y + us_ref[base - hist + t:base - hist + t + tt, :] * cw_ref[t:t + 1, :]
    o_ref[...] = ((cb_ref[...] * y) * _silu(cz_ref[...])).astype(o_ref.dtype)
    tail = us_ref[base + tt - hist:base + tt, :]
    us_ref[base - hist:base, :] = tail

    @pl.when(i == pl.num_programs(1) - 1)
    def _():
        tail_ref[0] = tail


def _sc(proj, conv_w, prefix, *, b, t, cw, c_col0):
    kw = conv_w.shape[0]
    tt = _pick(t, (256, 128, 64, 32, 16, 8))
    nt = t // tt
    cb = c_col0 // cw
    assert c_col0 % cw == 0
    kern = functools.partial(_sc_kernel, tt=tt, kw=kw)
    col = lambda off: pl.BlockSpec((tt, cw), lambda bi, i, off=off: (bi * nt + i, cb + off))
    return pl.pallas_call(
        kern,
        grid=(b, nt),
        in_specs=[
            col(0), col(1), col(2), col(3),
            pl.BlockSpec((kw, cw), lambda bi, i: (0, 0)),
            pl.BlockSpec((1, kw - 1, cw), lambda bi, i: (bi, 0, 0)),
        ],
        out_specs=[
            pl.BlockSpec((tt, cw), lambda bi, i: (bi * nt + i, 0)),
            pl.BlockSpec((1, kw - 1, cw), lambda bi, i: (bi, 0, 0)),
        ],
        out_shape=[jax.ShapeDtypeStruct((b * t, cw), BF16),
                   jax.ShapeDtypeStruct((b, kw - 1, cw), F32)],
        scratch_shapes=[pltpu.VMEM((tt + 8, cw), F32)],
        compiler_params=_cparams(("parallel", "arbitrary")),
        name="sc",
    )(proj, proj, proj, proj, conv_w, prefix)


def _outproj_kernel(a_ref, b_ref, c_ref, wa_ref, wb_ref, wc_ref, o_ref):
    o_ref[...] = (_dot(a_ref[...], wa_ref[...]) + _dot(b_ref[...], wb_ref[...])
                  + _dot(c_ref[...], wc_ref[...]))


def _outproj(oa, ob, oc, wa, wb, wc):
    m = oa.shape[0]
    d = wa.shape[1]
    tm = _pick(m, (1024, 512, 256, 128, 64, 32, 16))
    tn = _pick(d, (512, 256, 128))
    return pl.pallas_call(
        _outproj_kernel,
        grid=(m // tm, d // tn),
        in_specs=[
            pl.BlockSpec((tm, oa.shape[1]), lambda i, j: (i, 0)),
            pl.BlockSpec((tm, ob.shape[1]), lambda i, j: (i, 0)),
            pl.BlockSpec((tm, oc.shape[1]), lambda i, j: (i, 0)),
            pl.BlockSpec((wa.shape[0], tn), lambda i, j: (0, j)),
            pl.BlockSpec((wb.shape[0], tn), lambda i, j: (0, j)),
            pl.BlockSpec((wc.shape[0], tn), lambda i, j: (0, j)),
        ],
        out_specs=pl.BlockSpec((tm, tn), lambda i, j: (i, j)),
        out_shape=jax.ShapeDtypeStruct((m, d), F32),
        compiler_params=_cparams(("parallel", "parallel")),
        name="outproj",
    )(oa, ob, oc, wa, wb, wc)


def _post_kernel(x_ref, y_ref, g_ref, o_ref):
    y = y_ref[...]
    ms = jnp.mean(y * y, axis=-1, keepdims=True)
    o_ref[...] = x_ref[...] + (y * lax.rsqrt(ms + EPS)) * g_ref[...]


def _post(x2d, y2d, g_row):
    m, d = x2d.shape
    tm = _pick(m, (256, 128, 64, 32, 16, 8))
    spec = pl.BlockSpec((tm, d), lambda i: (i, 0))
    return pl.pallas_call(
        _post_kernel,
        grid=(m // tm,),
        in_specs=[spec, spec, pl.BlockSpec((1, d), lambda i: (0, 0))],
        out_specs=spec,
        out_shape=jax.ShapeDtypeStruct((m, d), F32),
        compiler_params=_cparams(("parallel",)),
        name="post",
    )(x2d, y2d, g_row)


def _layer(x, lw, consts, *, nh_g, nh_s, cw, sb_past, gdn_s0, gdn_conv0, sc_conv0):
    b, t, d = x.shape
    gw, sw = nh_g * HEAD_DIM, nh_s * HEAD_DIM
    x2d = x.reshape(b * t, d)
    proj, og = _proj(x2d, lw["g_pre"], lw["w_main"], lw["w_gate"])

    t_pad = -(-t // CHUNK) * CHUNK
    qkv, bx, gx = _gdn_pre(proj, og, lw["gdn_conv_w"], gdn_conv0, lw["alog"], lw["dtb"],
                           consts["expand"], consts["linc"], b=b, t=t, t_out=t_pad, nh=nh_g)
    o_a, s_new = _gdn(qkv, bx, gx, proj, gdn_s0, lw["gn"], b=b, t_pad=t_pad, t_valid=t,
                      nh=nh_g, az_col0=3 * gw)

    q_col0 = 4 * gw
    if sb_past is None:
        o_b = _sb_prompt(proj, consts["tri"], b=b, t=t, nh=nh_s, q_col0=q_col0)
    else:
        o_b = _sb_sample(proj, sb_past[0], sb_past[1], sb_past[2], consts["tri_new"], consts["tri"],
                         b=b, t=t, nh=nh_s, q_col0=q_col0)

    c_col0 = 4 * gw + 4 * sw
    o_c, sc_tail = _sc(proj, lw["sc_conv_w"], sc_conv0, b=b, t=t, cw=cw, c_col0=c_col0)

    y = _outproj(o_a, o_b, o_c, lw["w_out_a"], lw["w_out_b"], lw["w_out_c"])
    x_new = _post(x2d, y, lw["g_post"]).reshape(b, t, d)

    kcol = q_col0 + sw
    proj3 = proj.reshape(b, t, proj.shape[1])
    kb = proj3[:, :, kcol:kcol + sw].reshape(b, t, nh_s, HEAD_DIM)
    vb = proj3[:, :, kcol + sw:kcol + 2 * sw].reshape(b, t, nh_s, HEAD_DIM)
    kw = lw["gdn_conv_w"].shape[0]
    conv_new = proj3[:, t - (kw - 1):, :3 * gw]
    return x_new, (kb, vb, s_new, conv_new, sc_tail)


def _tri(n):
    r = jnp.arange(n)
    return (r[:, None] > r[None, :]).astype(BF16)


def kernel(x_prompt, x_sample, cache_sb_k, cache_sb_v, state_gdn, state_gdn_conv, state_sc_conv,
           w_in, w_out, norm_pre, norm_post, gdn_conv_w, gdn_a_log, gdn_dt_bias, gdn_norm, sc_conv_w):
    depth, d, _ = w_in.shape
    nh_g = gdn_a_log.shape[1]
    nh_s = cache_sb_k.shape[3]
    cw = sc_conv_w.shape[2]
    gw, sw = nh_g * HEAD_DIM, nh_s * HEAD_DIM
    bp, tp, _ = x_prompt.shape
    bs, ts, _ = x_sample.shape
    past = cache_sb_k.shape[2]
    kw_g = gdn_conv_w.shape[1]
    kw_s = sc_conv_w.shape[1]
    assert nh_g <= GATE_LANES and tp >= kw_g and ts >= kw_g and tp % CHUNK == 0

    g0 = 4 * gw
    w_main = jnp.concatenate([w_in[:, :, :g0], w_in[:, :, g0 + 2 * nh_g:]], axis=2).astype(BF16)
    zpad = jnp.zeros((depth, d, GATE_LANES - nh_g), w_in.dtype)
    w_gate = jnp.concatenate([w_in[:, :, g0:g0 + nh_g], zpad,
                              w_in[:, :, g0 + nh_g:g0 + 2 * nh_g], zpad], axis=2).astype(BF16)
    w_out_b16 = w_out.astype(BF16)
    lane_pad = lambda a: jnp.pad(a.astype(F32), ((0, 0), (0, GATE_LANES - nh_g)))[:, None, :]
    alog = lane_pad(gdn_a_log)
    dtb = lane_pad(gdn_dt_bias)

    lane_head = jnp.arange(gw) // HEAD_DIM
    tq = _pick(tp, (256, 128, 64, 32, 16, 8))
    tk = _pick(past, (256, 128, 64, 32, 16, 8))
    r = jnp.arange(CHUNK)
    consts = {
        "expand": (jnp.arange(GATE_LANES)[:, None] == lane_head[None, :]).astype(BF16),
        "linc": (r[:, None] >= r[None, :]).astype(BF16),
        "tri": _tri(tq),
    }
    consts_s = dict(consts, tri=_tri(tk), tri_new=_tri(ts))

    s0_p = jnp.zeros((bp, nh_g, HEAD_DIM, HEAD_DIM), F32)
    gconv0_p = jnp.zeros((bp, kw_g - 1, 3 * gw), F32)
    sconv0_p = jnp.zeros((bp, kw_s - 1, cw), F32)

    kpast = cache_sb_k.reshape(depth, bs * past, sw).astype(BF16)
    vpast = cache_sb_v.reshape(depth, bs * past, sw).astype(BF16)

    yp, ys = x_prompt, x_sample
    new_p = ([], [], [], [], [])
    new_s = ([], [], [], [], [])
    for l in range(depth):
        lw = {
            "g_pre": norm_pre[l][None, :], "g_post": norm_post[l][None, :],
            "w_main": w_main[l], "w_gate": w_gate[l],
            "w_out_a": w_out_b16[l, :gw], "w_out_b": w_out_b16[l, gw:gw + sw], "w_out_c": w_out_b16[l, gw + sw:],
            "gdn_conv_w": gdn_conv_w[l], "alog": alog[l], "dtb": dtb[l], "gn": gdn_norm[l][None, :],
            "sc_conv_w": sc_conv_w[l],
        }
        yp, st_p = _layer(yp, lw, consts, nh_g=nh_g, nh_s=nh_s, cw=cw, sb_past=None,
                          gdn_s0=s0_p, gdn_conv0=gconv0_p, sc_conv0=sconv0_p)
        ys, st_s = _layer(ys, lw, consts_s, nh_g=nh_g, nh_s=nh_s, cw=cw, sb_past=(kpast, vpast, l),
                          gdn_s0=state_gdn[l], gdn_conv0=state_gdn_conv[l], sc_conv0=state_sc_conv[l])
        for i in range(5):
            new_p[i].append(st_p[i])
            new_s[i].append(st_s[i])
    outs_p = [jnp.stack(a, axis=0) for a in new_p]
    outs_s = [jnp.stack(a, axis=0) for a in new_s]
    return (yp, ys, *outs_p, *outs_s)
```

```python
import functools

import jax
import jax.numpy as jnp
from jax import lax
from jax.experimental import pallas as pl
from jax.experimental.pallas import tpu as pltpu

F32 = jnp.float32
BF16 = jnp.bfloat16

HEAD_DIM = 128
CHUNK = 64
EPS = 1e-6
GATE_LANES = 128
VMEM_LIMIT = 56 * 1024 * 1024
OUTPROJ_VMEM_LIMIT = 58 * 1024 * 1024
GDN_HEADS_PER_STEP = (12, 6, 4, 3, 2, 1)
GDN_CHUNKS_PER_STEP = (2, 1)
SB_HEADS_PER_STEP = (6, 4, 3, 2, 1)


def _cparams(sem, vmem=VMEM_LIMIT):
    return pltpu.CompilerParams(dimension_semantics=sem, vmem_limit_bytes=vmem)


def _pick(n, prefs):
    for p in prefs:
        if n % p == 0:
            return p
    return n


def _dot(a, b):
    return jnp.dot(a, b, preferred_element_type=F32)


def _dot_nt(a, b):
    return lax.dot_general(a, b, (((1,), (1,)), ((), ())), preferred_element_type=F32)


def _dot_tn(a, b):
    return lax.dot_general(a, b, (((0,), (0,)), ((), ())), preferred_element_type=F32)


def _split2(x):
    hi = x.astype(BF16)
    lo = (x - hi.astype(F32)).astype(BF16)
    return hi, lo


def _split3(x):
    hi = x.astype(BF16)
    r = x - hi.astype(F32)
    mid = r.astype(BF16)
    lo = (r - mid.astype(F32)).astype(BF16)
    return hi, mid, lo


def _sigmoid(x):
    return 1.0 / (1.0 + jnp.exp(-x))


def _silu(x):
    return x * _sigmoid(x)


def _softplus(x):
    return jnp.maximum(x, 0.0) + jnp.log1p(jnp.exp(-jnp.abs(x)))


def _repack_kernel(w_ref, wm_ref, wg_ref, *, g0, nh):
    wm_ref[0, :, :g0] = w_ref[0, :, :g0].astype(BF16)
    wm_ref[0, :, g0:] = w_ref[0, :, g0 + 2 * nh:].astype(BF16)
    wg_ref[0] = jnp.zeros(wg_ref.shape[1:], BF16)
    wg_ref[0, :, :nh] = w_ref[0, :, g0:g0 + nh].astype(BF16)
    wg_ref[0, :, GATE_LANES:GATE_LANES + nh] = w_ref[0, :, g0 + nh:g0 + 2 * nh].astype(BF16)


def _repack(w_in, *, g0, nh):
    depth, d, d_in = w_in.shape
    n = d_in - 2 * nh
    tr = _pick(d, (128, 64, 32, 16))
    kern = functools.partial(_repack_kernel, g0=g0, nh=nh)
    return pl.pallas_call(
        kern,
        grid=(depth, d // tr),
        in_specs=[pl.BlockSpec((1, tr, d_in), lambda l, i: (l, i, 0))],
        out_specs=[pl.BlockSpec((1, tr, n), lambda l, i: (l, i, 0)),
                   pl.BlockSpec((1, tr, 2 * GATE_LANES), lambda l, i: (l, i, 0))],
        out_shape=[jax.ShapeDtypeStruct((depth, d, n), BF16),
                   jax.ShapeDtypeStruct((depth, d, 2 * GATE_LANES), BF16)],
        compiler_params=_cparams(("parallel", "parallel")),
        name="repack",
    )(w_in)


def _proj_kernel(x_ref, g_ref, w_ref, wg_ref, o_ref, og_ref, h_ref):
    @pl.when(pl.program_id(1) == 0)
    def _():
        x = x_ref[...]
        ms = jnp.mean(x * x, axis=-1, keepdims=True)
        h = ((x * lax.rsqrt(ms + EPS)) * g_ref[...]).astype(BF16)
        h_ref[...] = h
        og_ref[...] = _dot(h, wg_ref[...])

    o_ref[...] = _dot(h_ref[...], w_ref[...])


def _proj(x2d, g_row, w_main, w_gate, layer):
    m, d = x2d.shape
    n = w_main.shape[2]
    tm = _pick(m, (512, 256, 128, 64, 32, 16, 8))
    tn = _pick(n, (1024, 512, 256, 128))
    return pl.pallas_call(
        _proj_kernel,
        grid=(m // tm, n // tn),
        in_specs=[
            pl.BlockSpec((tm, d), lambda i, j: (i, 0)),
            pl.BlockSpec((1, d), lambda i, j: (0, 0)),
            pl.BlockSpec((None, d, tn), lambda i, j: (layer, 0, j)),
            pl.BlockSpec((None, d, 2 * GATE_LANES), lambda i, j: (layer, 0, 0)),
        ],
        out_specs=[
            pl.BlockSpec((tm, tn), lambda i, j: (i, j)),
            pl.BlockSpec((tm, 2 * GATE_LANES), lambda i, j: (i, 0)),
        ],
        out_shape=[jax.ShapeDtypeStruct((m, n), F32),
                   jax.ShapeDtypeStruct((m, 2 * GATE_LANES), F32)],
        scratch_shapes=[pltpu.VMEM((tm, d), BF16)],
        compiler_params=_cparams(("parallel", "arbitrary")),
        name="proj",
    )(x2d, g_row, w_main, w_gate)


def _gdn_pre_kernel(x_ref, og_ref, cw_ref, pre_ref, alog_ref, dtb_ref, e_ref, linc_ref,
                    qkv_ref, bx_ref, gx_ref, xs_ref, *, tt, tt_out, nh, kw):
    i = pl.program_id(1)
    hist = kw - 1
    base = 8

    @pl.when(i == 0)
    def _():
        xs_ref[base - hist:base, :] = pre_ref[0]

    xs_ref[base:base + tt, :] = x_ref[...]
    for c in range(3 * nh):
        cs = slice(c * HEAD_DIM, (c + 1) * HEAD_DIM)
        y = xs_ref[base - hist:base - hist + tt, cs] * cw_ref[0:1, cs]
        for t in range(1, kw):
            y = y + xs_ref[base - hist + t:base - hist + t + tt, cs] * cw_ref[t:t + 1, cs]
        y = _silu(y)
        if c < 2 * nh:
            y = y * lax.rsqrt(jnp.sum(y * y, axis=-1, keepdims=True) + EPS)
        if c < nh:
            y = y * (HEAD_DIM ** -0.5)
        qkv_ref[0:tt, cs] = y
    if tt_out > tt:
        qkv_ref[tt:tt_out, :] = jnp.zeros((tt_out - tt, qkv_ref.shape[1]), F32)
    xs_ref[base - hist:base, :] = xs_ref[base + tt - hist:base + tt, :]

    og = og_ref[...]
    g = -jnp.exp(alog_ref[...]) * _softplus(og[:, :GATE_LANES] + dtb_ref[...])
    beta = _sigmoid(og[:, GATE_LANES:])
    if tt_out > tt:
        pad = jnp.zeros((tt_out - tt, GATE_LANES), F32)
        g = jnp.concatenate([g, pad], axis=0)
        beta = jnp.concatenate([beta, pad], axis=0)
    e = e_ref[...]
    linc = linc_ref[...]
    bx = None
    for p in _split2(beta):
        t = _dot(p, e)
        bx = t if bx is None else bx + t
    bx_ref[...] = bx
    for r in range(tt_out // CHUNK):
        rs = slice(r * CHUNK, (r + 1) * CHUNK)
        gc = None
        for p in _split3(g[rs]):
            t = _dot(linc, p)
            gc = t if gc is None else gc + t
        gcx = None
        for p in _split2(gc):
            t = _dot(p, e)
            gcx = t if gcx is None else gcx + t
        gx_ref[rs, :] = gcx


def _gdn_pre(proj, og, conv_w, prefix, alog_row, dtb_row, expand, linc, *, b, t, t_out, nh):
    gw = nh * HEAD_DIM
    kw = conv_w.shape[0]
    if t_out > t:
        tt, tt_out = t, t_out
    else:
        tt = _pick(t, (256, 128, 64))
        tt_out = tt
    nt = t // tt
    kern = functools.partial(_gdn_pre_kernel, tt=tt, tt_out=tt_out, nh=nh, kw=kw)
    return pl.pallas_call(
        kern,
        grid=(b, nt),
        in_specs=[
            pl.BlockSpec((tt, 3 * gw), lambda bi, i: (bi * nt + i, 0)),
            pl.BlockSpec((tt, 2 * GATE_LANES), lambda bi, i: (bi * nt + i, 0)),
            pl.BlockSpec((kw, 3 * gw), lambda bi, i: (0, 0)),
            pl.BlockSpec((1, kw - 1, 3 * gw), lambda bi, i: (bi, 0, 0)),
            pl.BlockSpec((1, GATE_LANES), lambda bi, i: (0, 0)),
            pl.BlockSpec((1, GATE_LANES), lambda bi, i: (0, 0)),
            pl.BlockSpec((GATE_LANES, gw), lambda bi, i: (0, 0)),
            pl.BlockSpec((CHUNK, CHUNK), lambda bi, i: (0, 0)),
        ],
        out_specs=[
            pl.BlockSpec((tt_out, 3 * gw), lambda bi, i: (bi * nt + i, 0)),
            pl.BlockSpec((tt_out, gw), lambda bi, i: (bi * nt + i, 0)),
            pl.BlockSpec((tt_out, gw), lambda bi, i: (bi * nt + i, 0)),
        ],
        out_shape=[jax.ShapeDtypeStruct((b * nt * tt_out, 3 * gw), F32),
                   jax.ShapeDtypeStruct((b * nt * tt_out, gw), F32),
                   jax.ShapeDtypeStruct((b * nt * tt_out, gw), F32)],
        scratch_shapes=[pltpu.VMEM((tt + 8, 3 * gw), F32)],
        compiler_params=_cparams(("parallel", "arbitrary")),
        name="gdn_pre",
    )(proj, og, conv_w, prefix, alog_row, dtb_row, expand, linc)


def _gdn_kernel(q_ref, k_ref, v_ref, bx_ref, gx_ref, az_ref, s0_ref, gn_ref,
                o_ref, sout_ref, s_ref, *, hg, cpb, t_valid):
    i = pl.program_id(2)
    c2 = 2 * CHUNK

    @pl.when(i == 0)
    def _():
        s_ref[...] = s0_ref[0]

    row = lax.broadcasted_iota(jnp.int32, (CHUNK, c2), 0)
    lane = lax.broadcasted_iota(jnp.int32, (CHUNK, c2), 1)
    col = jnp.where(lane >= CHUNK, lane - CHUNK, lane)
    incl = row >= col
    strict = row > col
    right = lane >= CHUNK
    eye_right = jnp.where(right & (row == col), 1.0, 0.0).astype(F32)
    onehot0 = jnp.where(lane == 0, 1.0, 0.0).astype(BF16)
    zrow_b = jnp.zeros((CHUNK, HEAD_DIM), BF16)

    units = [(c, h) for c in range(cpb) for h in range(hg)]
    rsl = lambda c: slice(c * CHUNK, (c + 1) * CHUNK)
    csl = lambda h: slice(h * HEAD_DIM, (h + 1) * HEAD_DIM)
    st = {}
    for u in units:
        c, h = u
        q = q_ref[rsl(c), csl(h)]
        k = k_ref[rsl(c), csl(h)]
        v = v_ref[rsl(c), csl(h)]
        bx = bx_ref[rsl(c), csl(h)]
        gx = gx_ref[rsl(c), csl(h)]
        ex = jnp.exp(gx)
        gl = gx[CHUNK - 1:CHUNK, :]
        kb = k * bx
        st[u] = dict(gx=gx, k=k, kb=kb, q=q, egl=jnp.exp(gl), kd=(k * jnp.exp(gl - gx)).astype(BF16),
                     qd=q * ex, rhs=jnp.concatenate([v * bx, kb * ex], axis=1).astype(BF16))

    for u in units:
        rt = None
        for p in _split2(st[u]["gx"]):
            t = _dot_nt(onehot0, jnp.concatenate([p, p], axis=0))
            rt = t if rt is None else rt + t
        st[u]["rt"] = rt
    for u in units:
        s = st[u]
        kbf = s["k"].astype(BF16)
        s["qk"] = _dot_nt(jnp.concatenate([s["kb"], s["q"]], axis=0).astype(BF16),
                          jnp.concatenate([kbf, kbf], axis=0))
    for u in units:
        s = st[u]
        dec = jnp.where(incl, jnp.exp(jnp.where(incl, s["gx"] - s["rt"], 0.0)), 0.0)
        lmat = jnp.where(strict, s["qk"][:CHUNK] * dec, 0.0)
        s["attn"] = jnp.where(incl & (lane < CHUNK), s["qk"][CHUNK:] * dec, 0.0).astype(BF16)
        s["w"] = jnp.where(right, eye_right, -lmat)
    for _ in range(6):
        for u in units:
            s = st[u]
            wb = s["w"].astype(BF16)
            s["prod"] = _dot(wb, jnp.concatenate([wb, zrow_b], axis=0))
        for u in units:
            s = st[u]
            s["w"] = s["prod"] + jnp.where(right, s["w"], 0.0)
    for u in units:
        s = st[u]
        rhs = jnp.concatenate([jnp.zeros_like(s["rhs"]), s["rhs"]], axis=0)
        s["tu"] = _dot(s["w"].astype(BF16), rhs)

    state = [s_ref[h] for h in range(hg)]
    for c in range(cpb):
        nvalid = min(max(t_valid - c * CHUNK, 0), CHUNK)
        ws = []
        for h in range(hg):
            s = st[(c, h)]
            lhs = jnp.concatenate([s["tu"][:, HEAD_DIM:], s["qd"]], axis=0).astype(BF16)
            ws.append(_dot(lhs, state[h].astype(BF16)))
        vnb = [(st[(c, h)]["tu"][:, :HEAD_DIM] - ws[h][:CHUNK]).astype(BF16) for h in range(hg)]
        outs = []
        for h in range(hg):
            s = st[(c, h)]
            outs.append(ws[h][CHUNK:] + _dot(s["attn"], jnp.concatenate([vnb[h], zrow_b], axis=0)))
            state[h] = state[h] * s["egl"] + _dot_tn(s["kd"], vnb[h])
        if nvalid > 0:
            for h in range(hg):
                ov = outs[h][:nvalid]
                on = (ov * lax.rsqrt(jnp.mean(ov * ov, axis=-1, keepdims=True) + EPS)) * gn_ref[...]
                orow = slice(c * CHUNK, c * CHUNK + nvalid)
                o_ref[orow, csl(h)] = (on * _silu(az_ref[orow, csl(h)])).astype(o_ref.dtype)
    for h in range(hg):
        s_ref[h] = state[h]

    @pl.when(i == pl.num_programs(2) - 1)
    def _():
        sout_ref[0] = s_ref[...]


def _gdn(qkv, bx, gx, proj, s0, gn_row, *, b, t_pad, t_valid, nh, az_col0):
    gw = nh * HEAD_DIM
    hg = _pick(nh, GDN_HEADS_PER_STEP)
    ng = nh // hg
    if t_valid < t_pad:
        cpb = t_pad // CHUNK
    else:
        cpb = _pick(t_pad // CHUNK, GDN_CHUNKS_PER_STEP)
    rb = cpb * CHUNK
    nblk = t_pad // rb
    rv = min(rb, t_valid)
    assert nblk == 1 or rv == rb
    hw = hg * HEAD_DIM
    azb = az_col0 // hw
    assert az_col0 % hw == 0
    kern = functools.partial(_gdn_kernel, hg=hg, cpb=cpb, t_valid=rv)
    qspec = lambda off: pl.BlockSpec((rb, hw), lambda bi, g, i, off=off: (bi * nblk + i, off + g))
    return pl.pallas_call(
        kern,
        grid=(b, ng, nblk),
        in_specs=[
            qspec(0), qspec(ng), qspec(2 * ng),
            pl.BlockSpec((rb, hw), lambda bi, g, i: (bi * nblk + i, g)),
            pl.BlockSpec((rb, hw), lambda bi, g, i: (bi * nblk + i, g)),
            pl.BlockSpec((rv, hw), lambda bi, g, i: (bi * nblk + i, azb + g)),
            pl.BlockSpec((1, hg, HEAD_DIM, HEAD_DIM), lambda bi, g, i: (bi, g, 0, 0)),
            pl.BlockSpec((1, HEAD_DIM), lambda bi, g, i: (0, 0)),
        ],
        out_specs=[
            pl.BlockSpec((rv, hw), lambda bi, g, i: (bi * nblk + i, g)),
            pl.BlockSpec((1, hg, HEAD_DIM, HEAD_DIM), lambda bi, g, i: (bi, g, 0, 0)),
        ],
        out_shape=[jax.ShapeDtypeStruct((b * nblk * rv, gw), BF16),
                   jax.ShapeDtypeStruct((b, nh, HEAD_DIM, HEAD_DIM), F32)],
        scratch_shapes=[pltpu.VMEM((hg, HEAD_DIM, HEAD_DIM), F32)],
        compiler_params=_cparams(("parallel", "parallel", "arbitrary")),
        name="gdn",
    )(qkv, qkv, qkv, bx, gx, proj, s0, gn_row)


SB_Z2_SCALE = (HEAD_DIM ** -0.5) * 1.4426950408889634

def _neg_abs(x):
    bits = lax.bitcast_convert_type(x, jnp.uint32) | jnp.uint32(0x80000000)
    return lax.bitcast_convert_type(bits, F32)


def _lane_tile(x, n):
    if n <= HEAD_DIM:
        return x[:, :n]
    return jnp.concatenate([x] * (n // HEAD_DIM), axis=1)


def _sb_blocks(q_bfs, k_blks, v_blks, tri, drops, mask):
    z2s = [_dot_nt(q, k.astype(BF16)) for q, k in zip(q_bfs, k_blks)]
    lss, nlks = [], []
    for z2 in z2s:
        ls = jnp.minimum(z2, 0.0) - jnp.log2(1.0 + jnp.exp2(_neg_abs(z2)))
        lss.append(ls)
        nlk = z2 - ls
        if mask is not None:
            nlk = jnp.where(mask, nlk, 0.0)
        nlks.append(nlk)
    betw = [_dot(nlk.astype(BF16), tri) for nlk in nlks]
    pvs = []
    for ls, bt, drop, v in zip(lss, betw, drops, v_blks):
        a = jnp.exp2(ls - bt - _lane_tile(drop, bt.shape[1]))
        if mask is not None:
            a = jnp.where(mask, a, 0.0)
        pvs.append(_dot(a.astype(BF16), v.astype(BF16)))
    return pvs, [jnp.sum(nlk, axis=-1, keepdims=True) for nlk in nlks]


def _sb_sweep(q_ref, z_ref, o_ref, acc_ref, drop_ref, first, blocks, nblocks, tri, *, hb):
    hsl = lambda h: slice(h * HEAD_DIM, (h + 1) * HEAD_DIM)
    q_bfs = [(q_ref[:, hsl(h)] * SB_Z2_SCALE).astype(BF16) for h in range(hb)]
    rows = q_ref.shape[0]
    k0, v0, tri0, mask0 = first
    pvs, sums = _sb_blocks(q_bfs, k0, v0, tri0, [jnp.zeros((rows, HEAD_DIM), F32)] * hb, mask0)
    for h in range(hb):
        acc_ref[:, hsl(h)] = pvs[h]
        drop_ref[:, hsl(h)] = jnp.broadcast_to(sums[h], (rows, HEAD_DIM))

    def body(step, carry):
        ks, vs = blocks(step)
        drops = [drop_ref[:, hsl(h)] for h in range(hb)]
        pvs, sums = _sb_blocks(q_bfs, ks, vs, tri, drops, None)
        for h in range(hb):
            acc_ref[:, hsl(h)] += pvs[h]
            drop_ref[:, hsl(h)] += jnp.broadcast_to(sums[h], (rows, HEAD_DIM))
        return carry

    lax.fori_loop(0, nblocks, body, 0)
    o_ref[...] = (acc_ref[...] * _silu(z_ref[...])).astype(o_ref.dtype)


def _sb_prompt_kernel(q_ref, k_ref, v_ref, z_ref, tri_ref, o_ref, acc_ref, drop_ref, *, tq, hb):
    qi = pl.program_id(2)
    hsl = lambda h: slice(h * HEAD_DIM, (h + 1) * HEAD_DIM)
    tri = tri_ref[...]
    row = lax.broadcasted_iota(jnp.int32, (tq, tq), 0)
    col = lax.broadcasted_iota(jnp.int32, (tq, tq), 1)

    def load(r0):
        return ([k_ref[pl.ds(r0, tq), hsl(h)] for h in range(hb)],
                [v_ref[pl.ds(r0, tq), hsl(h)] for h in range(hb)])

    k0, v0 = load(pl.multiple_of(qi * tq, tq))
    blocks = lambda step: load(pl.multiple_of((qi - 1 - step) * tq, tq))
    _sb_sweep(q_ref, z_ref, o_ref, acc_ref, drop_ref, (k0, v0, tri, col < row), blocks, qi, tri, hb=hb)


def _sb_heads_per_step(nh, q_col0):
    for hb in SB_HEADS_PER_STEP:
        if nh % hb == 0 and (q_col0 // HEAD_DIM) % hb == 0:
            return hb
    return 1


def _sb_prompt(proj, tri, *, b, t, nh, q_col0):
    sw = nh * HEAD_DIM
    tq = tri.shape[0]
    nq = t // tq
    hb = _sb_heads_per_step(nh, q_col0)
    hw = hb * HEAD_DIM
    ng = nh // hb
    qb = q_col0 // hw
    kern = functools.partial(_sb_prompt_kernel, tq=tq, hb=hb)

    def rows(off):
        return pl.BlockSpec((tq, hw), lambda bi, g, i: (bi * nq + i, qb + off * ng + g))

    def seq(off):
        return pl.BlockSpec((t, hw), lambda bi, g, i: (bi, qb + off * ng + g), pipeline_mode=pl.Buffered(1))

    return pl.pallas_call(
        kern,
        grid=(b, ng, nq),
        in_specs=[rows(0), seq(1), seq(2), rows(3), pl.BlockSpec((tq, tq), lambda bi, g, i: (0, 0))],
        out_specs=pl.BlockSpec((tq, hw), lambda bi, g, i: (bi * nq + i, g)),
        out_shape=jax.ShapeDtypeStruct((b * t, sw), BF16),
        scratch_shapes=[pltpu.VMEM((tq, hw), F32), pltpu.VMEM((tq, hw), F32)],
        compiler_params=_cparams(("parallel", "parallel", "arbitrary")),
        name="sb_prompt",
    )(proj, proj, proj, proj, tri)


def _sb_sample_kernel(q_ref, k_ref, v_ref, z_ref, kp_ref, vp_ref, trin_ref, tri_ref, o_ref,
                      acc_ref, drop_ref, *, t, tk, npast, hb):
    hsl = lambda h: slice(h * HEAD_DIM, (h + 1) * HEAD_DIM)
    row = lax.broadcasted_iota(jnp.int32, (t, t), 0)
    col = lax.broadcasted_iota(jnp.int32, (t, t), 1)
    k0 = [k_ref[:, hsl(h)] for h in range(hb)]
    v0 = [v_ref[:, hsl(h)] for h in range(hb)]

    def blocks(step):
        r0 = pl.multiple_of((npast - 1 - step) * tk, tk)
        return ([kp_ref[h, pl.ds(r0, tk), :] for h in range(hb)],
                [vp_ref[h, pl.ds(r0, tk), :] for h in range(hb)])

    _sb_sweep(q_ref, z_ref, o_ref, acc_ref, drop_ref, (k0, v0, trin_ref[...], col < row), blocks, npast,
              tri_ref[...], hb=hb)


def _sb_sample(proj, k_past, v_past, layer, tri_new, tri, *, b, t, nh, q_col0):
    sw = nh * HEAD_DIM
    past = k_past.shape[3]
    tk = tri.shape[0]
    assert past % tk == 0
    hb = _sb_heads_per_step(nh, q_col0)
    hw = hb * HEAD_DIM
    ng = nh // hb
    qb = q_col0 // hw
    kern = functools.partial(_sb_sample_kernel, t=t, tk=tk, npast=past // tk, hb=hb)

    def cur(off):
        return pl.BlockSpec((t, hw), lambda bi, g: (bi, qb + off * ng + g))

    old = pl.BlockSpec((None, None, hb, past, HEAD_DIM), lambda bi, g: (layer, bi, g, 0, 0))
    return pl.pallas_call(
        kern,
        grid=(b, ng),
        in_specs=[
            cur(0), cur(1), cur(2), cur(3), old, old,
            pl.BlockSpec((t, t), lambda bi, g: (0, 0)),
            pl.BlockSpec((tk, tk), lambda bi, g: (0, 0)),
        ],
        out_specs=pl.BlockSpec((t, hw), lambda bi, g: (bi, g)),
        out_shape=jax.ShapeDtypeStruct((b * t, sw), BF16),
        scratch_shapes=[pltpu.VMEM((t, hw), F32), pltpu.VMEM((t, hw), F32)],
        compiler_params=_cparams(("parallel", "parallel")),
        name="sb_sample",
    )(proj, proj, proj, proj, k_past, v_past, tri_new, tri)


def _sc_kernel(cb_ref, cc_ref, ch_ref, cz_ref, cw_ref, pre_ref, o_ref, tail_ref, us_ref, *, tt, kw):
    i = pl.program_id(1)
    hist = kw - 1
    base = 8

    @pl.when(i == 0)
    def _():
        us_ref[base - hist:base, :] = pre_ref[0]

    us_ref[base:base + tt, :] = cc_ref[...] * ch_ref[...]
    y = us_ref[base - hist:base - hist + tt, :] * cw_ref[0:1, :]
    for t in range(1, kw):
        y = y + us_ref[base - hist + t:base - hist + t + tt, :] * cw_ref[t:t + 1, :]
    o_ref[...] = ((cb_ref[...] * y) * _silu(cz_ref[...])).astype(o_ref.dtype)
    tail = us_ref[base + tt - hist:base + tt, :]
    us_ref[base - hist:base, :] = tail

    @pl.when(i == pl.num_programs(1) - 1)
    def _():
        tail_ref[0] = tail


def _sc(proj, conv_w, prefix, *, b, t, cw, c_col0):
    kw = conv_w.shape[0]
    tt = _pick(t, (256, 128, 64, 32, 16, 8))
    nt = t // tt
    cb = c_col0 // cw
    assert c_col0 % cw == 0
    kern = functools.partial(_sc_kernel, tt=tt, kw=kw)
    col = lambda off: pl.BlockSpec((tt, cw), lambda bi, i, off=off: (bi * nt + i, cb + off))
    return pl.pallas_call(
        kern,
        grid=(b, nt),
        in_specs=[
            col(0), col(1), col(2), col(3),
            pl.BlockSpec((kw, cw), lambda bi, i: (0, 0)),
            pl.BlockSpec((1, kw - 1, cw), lambda bi, i: (bi, 0, 0)),
        ],
        out_specs=[
            pl.BlockSpec((tt, cw), lambda bi, i: (bi * nt + i, 0)),
            pl.BlockSpec((1, kw - 1, cw), lambda bi, i: (bi, 0, 0)),
        ],
        out_shape=[jax.ShapeDtypeStruct((b * t, cw), BF16),
                   jax.ShapeDtypeStruct((b, kw - 1, cw), F32)],
        scratch_shapes=[pltpu.VMEM((tt + 8, cw), F32)],
        compiler_params=_cparams(("parallel", "arbitrary")),
        name="sc",
    )(proj, proj, proj, proj, conv_w, prefix)


def _outproj_kernel(a_ref, b_ref, c_ref, wa_ref, wb_ref, wc_ref, x_ref, g_ref, o_ref, *, tn, nj):
    j = pl.program_id(1)
    y = _dot(a_ref[...], wa_ref[...]) + _dot(b_ref[...], wb_ref[...]) + _dot(c_ref[...], wc_ref[...])
    o_ref[:, pl.ds(pl.multiple_of(j * tn, tn), tn)] = y

    @pl.when(j == nj - 1)
    def _():
        yy = o_ref[...]
        ms = jnp.mean(yy * yy, axis=-1, keepdims=True)
        o_ref[...] = x_ref[...] + (yy * lax.rsqrt(ms + EPS)) * g_ref[...]


def _outproj(oa, ob, oc, w_out, layer, x2d, g_row):
    m = oa.shape[0]
    d = w_out.shape[2]
    ka, kb, kc = oa.shape[1], ob.shape[1], oc.shape[1]
    assert ka % kb == 0 and (ka + kb) % kc == 0
    tm = _pick(m, (512, 256, 128, 64, 32, 16))
    tn = _pick(d, (512, 256, 128))
    kern = functools.partial(_outproj_kernel, tn=tn, nj=d // tn)
    return pl.pallas_call(
        kern,
        grid=(m // tm, d // tn),
        in_specs=[
            pl.BlockSpec((tm, ka), lambda i, j: (i, 0)),
            pl.BlockSpec((tm, kb), lambda i, j: (i, 0)),
            pl.BlockSpec((tm, kc), lambda i, j: (i, 0)),
            pl.BlockSpec((None, ka, tn), lambda i, j: (layer, 0, j)),
            pl.BlockSpec((None, kb, tn), lambda i, j: (layer, ka // kb, j)),
            pl.BlockSpec((None, kc, tn), lambda i, j: (layer, (ka + kb) // kc, j)),
            pl.BlockSpec((tm, d), lambda i, j: (i, 0)),
            pl.BlockSpec((1, d), lambda i, j: (0, 0)),
        ],
        out_specs=pl.BlockSpec((tm, d), lambda i, j: (i, 0)),
        out_shape=jax.ShapeDtypeStruct((m, d), F32),
        compiler_params=_cparams(("parallel", "arbitrary"), OUTPROJ_VMEM_LIMIT),
        name="outproj",
    )(oa, ob, oc, w_out, w_out, w_out, x2d, g_row)


def _layer(x, lw, consts, *, nh_g, nh_s, cw, sb_past, gdn_s0, gdn_conv0, sc_conv0):
    b, t, d = x.shape
    gw, sw = nh_g * HEAD_DIM, nh_s * HEAD_DIM
    x2d = x.reshape(b * t, d)
    proj, og = _proj(x2d, lw["g_pre"], lw["w_main"], lw["w_gate"], lw["layer"])

    t_pad = -(-t // CHUNK) * CHUNK
    qkv, bx, gx = _gdn_pre(proj, og, lw["gdn_conv_w"], gdn_conv0, lw["alog"], lw["dtb"],
                           consts["expand"], consts["linc"], b=b, t=t, t_out=t_pad, nh=nh_g)
    o_a, s_new = _gdn(qkv, bx, gx, proj, gdn_s0, lw["gn"], b=b, t_pad=t_pad, t_valid=t,
                      nh=nh_g, az_col0=3 * gw)

    q_col0 = 4 * gw
    if sb_past is None:
        o_b = _sb_prompt(proj, consts["tri"], b=b, t=t, nh=nh_s, q_col0=q_col0)
    else:
        o_b = _sb_sample(proj, sb_past[0], sb_past[1], sb_past[2], consts["tri_new"], consts["tri"],
                         b=b, t=t, nh=nh_s, q_col0=q_col0)

    c_col0 = 4 * gw + 4 * sw
    o_c, sc_tail = _sc(proj, lw["sc_conv_w"], sc_conv0, b=b, t=t, cw=cw, c_col0=c_col0)

    x_new = _outproj(o_a, o_b, o_c, lw["w_out"], lw["layer"], x2d, lw["g_post"]).reshape(b, t, d)

    kcol = q_col0 + sw
    proj3 = proj.reshape(b, t, proj.shape[1])
    kb = proj3[:, :, kcol:kcol + sw].reshape(b, t, nh_s, HEAD_DIM)
    vb = proj3[:, :, kcol + sw:kcol + 2 * sw].reshape(b, t, nh_s, HEAD_DIM)
    kw = lw["gdn_conv_w"].shape[0]
    conv_new = proj3[:, t - (kw - 1):, :3 * gw]
    return x_new, (kb, vb, s_new, conv_new, sc_tail)


def _tri(n):
    r = jnp.arange(n)
    return (r[:, None] > r[None, :]).astype(BF16)


def kernel(x_prompt, x_sample, cache_sb_k, cache_sb_v, state_gdn, state_gdn_conv, state_sc_conv,
           w_in, w_out, norm_pre, norm_post, gdn_conv_w, gdn_a_log, gdn_dt_bias, gdn_norm, sc_conv_w):
    depth, d, _ = w_in.shape
    nh_g = gdn_a_log.shape[1]
    nh_s = cache_sb_k.shape[3]
    cw = sc_conv_w.shape[2]
    gw, sw = nh_g * HEAD_DIM, nh_s * HEAD_DIM
    bp, tp, _ = x_prompt.shape
    bs, ts, _ = x_sample.shape
    past = cache_sb_k.shape[2]
    kw_g = gdn_conv_w.shape[1]
    kw_s = sc_conv_w.shape[1]
    assert nh_g <= GATE_LANES and tp >= kw_g and ts >= kw_g and tp % CHUNK == 0

    w_main, w_gate = _repack(w_in, g0=4 * gw, nh=nh_g)
    w_out_b16 = w_out.astype(BF16)
    lane_pad = lambda a: jnp.pad(a.astype(F32), ((0, 0), (0, GATE_LANES - nh_g)))[:, None, :]
    alog = lane_pad(gdn_a_log)
    dtb = lane_pad(gdn_dt_bias)

    lane_head = jnp.arange(gw) // HEAD_DIM
    tq = _pick(tp, (256, 128, 64, 32, 16, 8))
    tk = _pick(past, (256, 128, 64, 32, 16, 8))
    r = jnp.arange(CHUNK)
    consts = {
        "expand": (jnp.arange(GATE_LANES)[:, None] == lane_head[None, :]).astype(BF16),
        "linc": (r[:, None] >= r[None, :]).astype(BF16),
        "tri": _tri(tq),
    }
    consts_s = dict(consts, tri=_tri(tk), tri_new=_tri(ts))

    s0_p = jnp.zeros((bp, nh_g, HEAD_DIM, HEAD_DIM), F32)
    gconv0_p = jnp.zeros((bp, kw_g - 1, 3 * gw), F32)
    sconv0_p = jnp.zeros((bp, kw_s - 1, cw), F32)

    kpast = jnp.transpose(cache_sb_k, (0, 1, 3, 2, 4)).astype(BF16)
    vpast = jnp.transpose(cache_sb_v, (0, 1, 3, 2, 4)).astype(BF16)

    yp, ys = x_prompt, x_sample
    new_p = ([], [], [], [], [])
    new_s = ([], [], [], [], [])
    for l in range(depth):
        lw = {
            "layer": l, "g_pre": norm_pre[l][None, :], "g_post": norm_post[l][None, :],
            "w_main": w_main, "w_gate": w_gate, "w_out": w_out_b16,
            "gdn_conv_w": gdn_conv_w[l], "alog": alog[l], "dtb": dtb[l], "gn": gdn_norm[l][None, :],
            "sc_conv_w": sc_conv_w[l],
        }
        yp, st_p = _layer(yp, lw, consts, nh_g=nh_g, nh_s=nh_s, cw=cw, sb_past=None,
                          gdn_s0=s0_p, gdn_conv0=gconv0_p, sc_conv0=sconv0_p)
        ys, st_s = _layer(ys, lw, consts_s, nh_g=nh_g, nh_s=nh_s, cw=cw, sb_past=(kpast, vpast, l),
                          gdn_s0=state_gdn[l], gdn_conv0=state_gdn_conv[l], sc_conv0=state_sc_conv[l])
        for i in range(5):
            new_p[i].append(st_p[i])
            new_s[i].append(st_s[i])
    outs_p = [jnp.stack(a, axis=0) for a in new_p]
    outs_s = [jnp.stack(a, axis=0) for a in new_s]
    return (yp, ys, *outs_p, *outs_s)
```

```python
import functools

import jax
import jax.numpy as jnp
from jax import lax
from jax.experimental import pallas as pl
from jax.experimental.pallas import tpu as pltpu

F32 = jnp.float32
BF16 = jnp.bfloat16

HEAD_DIM = 128
CHUNK = 64
EPS = 1e-6
GATE_LANES = 128
VMEM_LIMIT = 56 * 1024 * 1024
OUTPROJ_ROWS = (128, 64, 32, 16)
GDN_HEADS_PER_STEP = (12, 6, 4, 3, 2, 1)
GDN_CHUNKS_PER_STEP = (2, 1)
SB_HEADS_PER_STEP = (6, 4, 3, 2, 1)


def _cparams(sem, vmem=VMEM_LIMIT):
    return pltpu.CompilerParams(dimension_semantics=sem, vmem_limit_bytes=vmem)


def _pick(n, prefs):
    for p in prefs:
        if n % p == 0:
            return p
    return n


def _dot(a, b):
    return jnp.dot(a, b, preferred_element_type=F32)


def _dot_nt(a, b):
    return lax.dot_general(a, b, (((1,), (1,)), ((), ())), preferred_element_type=F32)


def _dot_tn(a, b):
    return lax.dot_general(a, b, (((0,), (0,)), ((), ())), preferred_element_type=F32)


def _split2(x):
    hi = x.astype(BF16)
    lo = (x - hi.astype(F32)).astype(BF16)
    return hi, lo


def _split3(x):
    hi = x.astype(BF16)
    r = x - hi.astype(F32)
    mid = r.astype(BF16)
    lo = (r - mid.astype(F32)).astype(BF16)
    return hi, mid, lo


def _sigmoid(x):
    return 1.0 / (1.0 + jnp.exp(-x))


def _silu(x):
    return x * _sigmoid(x)


def _softplus(x):
    return jnp.maximum(x, 0.0) + jnp.log1p(jnp.exp(-jnp.abs(x)))


def _repack_kernel(w_ref, wm_ref, wg_ref, scr_ref, *, gi, nh, tr):
    i = pl.program_id(1)
    s = 2 * nh

    @pl.when(i < gi)
    def _():
        wm_ref[...] = w_ref[...].astype(BF16)

    @pl.when(i == gi)
    def _():
        wg_ref[...] = jnp.zeros(wg_ref.shape, F32)
        wg_ref[0:nh, :] = w_ref[0:nh, :]
        wg_ref[GATE_LANES:GATE_LANES + nh, :] = w_ref[nh:s, :]

    @pl.when(i > gi)
    def _():
        scr_ref[tr - s:tr, :] = w_ref[0:s, :]
        wm_ref[...] = scr_ref[...].astype(BF16)

    @pl.when((i >= gi) & (i < pl.num_programs(1) - 1))
    def _():
        scr_ref[0:tr - s, :] = w_ref[s:tr, :]


def _repack(w_in_t, *, g0, nh):
    depth, d_in, d = w_in_t.shape
    n = d_in - 2 * nh
    tr = next(t for t in (512, 256, 128, 64, 32, 16, 8) if g0 % t == 0 and n % t == 0 and t > 2 * nh)
    gi = g0 // tr
    kern = functools.partial(_repack_kernel, gi=gi, nh=nh, tr=tr)
    return pl.pallas_call(
        kern,
        grid=(depth, n // tr + 1),
        in_specs=[pl.BlockSpec((None, tr, d), lambda l, i: (l, i, 0))],
        out_specs=[pl.BlockSpec((None, tr, d), lambda l, i: (l, jnp.where(i < gi, i, i - 1), 0)),
                   pl.BlockSpec((None, 2 * GATE_LANES, d), lambda l, i: (l, 0, 0))],
        out_shape=[jax.ShapeDtypeStruct((depth, n, d), BF16),
                   jax.ShapeDtypeStruct((depth, 2 * GATE_LANES, d), F32)],
        scratch_shapes=[pltpu.VMEM((tr, d), F32)],
        compiler_params=_cparams(("parallel", "arbitrary")),
        name="repack",
    )(w_in_t)


def _proj_kernel(x_ref, g_ref, w_ref, wg_ref, o_ref, og_ref, h_ref):
    @pl.when(pl.program_id(1) == 0)
    def _():
        x = x_ref[...]
        ms = jnp.mean(x * x, axis=-1, keepdims=True)
        h = ((x * lax.rsqrt(ms + EPS)) * g_ref[...]).astype(BF16)
        h_ref[...] = h
        og_ref[...] = _dot_nt(h, wg_ref[...])

    o_ref[...] = _dot_nt(h_ref[...], w_ref[...])


def _proj(x2d, g_row, w_main, w_gate, layer):
    m, d = x2d.shape
    n = w_main.shape[1]
    tm = _pick(m, (512, 256, 128, 64, 32, 16, 8))
    tn = _pick(n, (1024, 512, 256, 128))
    return pl.pallas_call(
        _proj_kernel,
        grid=(m // tm, n // tn),
        in_specs=[
            pl.BlockSpec((tm, d), lambda i, j: (i, 0)),
            pl.BlockSpec((1, d), lambda i, j: (0, 0)),
            pl.BlockSpec((None, tn, d), lambda i, j: (layer, j, 0)),
            pl.BlockSpec((None, 2 * GATE_LANES, d), lambda i, j: (layer, 0, 0)),
        ],
        out_specs=[
            pl.BlockSpec((tm, tn), lambda i, j: (i, j)),
            pl.BlockSpec((tm, 2 * GATE_LANES), lambda i, j: (i, 0)),
        ],
        out_shape=[jax.ShapeDtypeStruct((m, n), F32),
                   jax.ShapeDtypeStruct((m, 2 * GATE_LANES), F32)],
        scratch_shapes=[pltpu.VMEM((tm, d), BF16)],
        compiler_params=_cparams(("parallel", "arbitrary")),
        name="proj",
    )(x2d, g_row, w_main, w_gate)


def _gdn_pre_kernel(x_ref, og_ref, cw_ref, pre_ref, alog_ref, dtb_ref, e_ref, linc_ref,
                    qkv_ref, bx_ref, gx_ref, xs_ref, *, tt, tt_out, nh, kw):
    i = pl.program_id(1)
    hist = kw - 1
    base = 8

    @pl.when(i == 0)
    def _():
        xs_ref[base - hist:base, :] = pre_ref[0]

    xs_ref[base:base + tt, :] = x_ref[...]
    for c in range(3 * nh):
        cs = slice(c * HEAD_DIM, (c + 1) * HEAD_DIM)
        y = xs_ref[base - hist:base - hist + tt, cs] * cw_ref[0:1, cs]
        for t in range(1, kw):
            y = y + xs_ref[base - hist + t:base - hist + t + tt, cs] * cw_ref[t:t + 1, cs]
        y = _silu(y)
        if c < 2 * nh:
            y = y * lax.rsqrt(jnp.sum(y * y, axis=-1, keepdims=True) + EPS)
        if c < nh:
            y = y * (HEAD_DIM ** -0.5)
        qkv_ref[0:tt, cs] = y
    if tt_out > tt:
        qkv_ref[tt:tt_out, :] = jnp.zeros((tt_out - tt, qkv_ref.shape[1]), F32)
    xs_ref[base - hist:base, :] = xs_ref[base + tt - hist:base + tt, :]

    og = og_ref[...]
    g = -jnp.exp(alog_ref[...]) * _softplus(og[:, :GATE_LANES] + dtb_ref[...])
    beta = _sigmoid(og[:, GATE_LANES:])
    if tt_out > tt:
        pad = jnp.zeros((tt_out - tt, GATE_LANES), F32)
        g = jnp.concatenate([g, pad], axis=0)
        beta = jnp.concatenate([beta, pad], axis=0)
    e = e_ref[...]
    linc = linc_ref[...]
    bx = None
    for p in _split2(beta):
        t = _dot(p, e)
        bx = t if bx is None else bx + t
    bx_ref[...] = bx
    for r in range(tt_out // CHUNK):
        rs = slice(r * CHUNK, (r + 1) * CHUNK)
        gc = None
        for p in _split3(g[rs]):
            t = _dot(linc, p)
            gc = t if gc is None else gc + t
        gcx = None
        for p in _split2(gc):
            t = _dot(p, e)
            gcx = t if gcx is None else gcx + t
        gx_ref[rs, :] = gcx


def _gdn_pre(proj, og, conv_w, prefix, alog_row, dtb_row, expand, linc, *, b, t, t_out, nh):
    gw = nh * HEAD_DIM
    kw = conv_w.shape[0]
    if t_out > t:
        tt, tt_out = t, t_out
    else:
        tt = _pick(t, (256, 128, 64))
        tt_out = tt
    nt = t // tt
    kern = functools.partial(_gdn_pre_kernel, tt=tt, tt_out=tt_out, nh=nh, kw=kw)
    return pl.pallas_call(
        kern,
        grid=(b, nt),
        in_specs=[
            pl.BlockSpec((tt, 3 * gw), lambda bi, i: (bi * nt + i, 0)),
            pl.BlockSpec((tt, 2 * GATE_LANES), lambda bi, i: (bi * nt + i, 0)),
            pl.BlockSpec((kw, 3 * gw), lambda bi, i: (0, 0)),
            pl.BlockSpec((1, kw - 1, 3 * gw), lambda bi, i: (bi, 0, 0)),
            pl.BlockSpec((1, GATE_LANES), lambda bi, i: (0, 0)),
            pl.BlockSpec((1, GATE_LANES), lambda bi, i: (0, 0)),
            pl.BlockSpec((GATE_LANES, gw), lambda bi, i: (0, 0)),
            pl.BlockSpec((CHUNK, CHUNK), lambda bi, i: (0, 0)),
        ],
        out_specs=[
            pl.BlockSpec((tt_out, 3 * gw), lambda bi, i: (bi * nt + i, 0)),
            pl.BlockSpec((tt_out, gw), lambda bi, i: (bi * nt + i, 0)),
            pl.BlockSpec((tt_out, gw), lambda bi, i: (bi * nt + i, 0)),
        ],
        out_shape=[jax.ShapeDtypeStruct((b * nt * tt_out, 3 * gw), F32),
                   jax.ShapeDtypeStruct((b * nt * tt_out, gw), F32),
                   jax.ShapeDtypeStruct((b * nt * tt_out, gw), F32)],
        scratch_shapes=[pltpu.VMEM((tt + 8, 3 * gw), F32)],
        compiler_params=_cparams(("parallel", "arbitrary")),
        name="gdn_pre",
    )(proj, og, conv_w, prefix, alog_row, dtb_row, expand, linc)


def _gdn_kernel(q_ref, k_ref, v_ref, bx_ref, gx_ref, az_ref, s0_ref, gn_ref,
                o_ref, sout_ref, s_ref, *, hg, cpb, t_valid):
    i = pl.program_id(2)
    c2 = 2 * CHUNK

    @pl.when(i == 0)
    def _():
        s_ref[...] = s0_ref[0]

    row = lax.broadcasted_iota(jnp.int32, (CHUNK, c2), 0)
    lane = lax.broadcasted_iota(jnp.int32, (CHUNK, c2), 1)
    col = jnp.where(lane >= CHUNK, lane - CHUNK, lane)
    incl = row >= col
    strict = row > col
    right = lane >= CHUNK
    eye_right = jnp.where(right & (row == col), 1.0, 0.0).astype(F32)
    onehot0 = jnp.where(lane == 0, 1.0, 0.0).astype(BF16)
    zrow_b = jnp.zeros((CHUNK, HEAD_DIM), BF16)

    units = [(c, h) for c in range(cpb) for h in range(hg)]
    rsl = lambda c: slice(c * CHUNK, (c + 1) * CHUNK)
    csl = lambda h: slice(h * HEAD_DIM, (h + 1) * HEAD_DIM)
    st = {}
    for u in units:
        c, h = u
        q = q_ref[rsl(c), csl(h)]
        k = k_ref[rsl(c), csl(h)]
        v = v_ref[rsl(c), csl(h)]
        bx = bx_ref[rsl(c), csl(h)]
        gx = gx_ref[rsl(c), csl(h)]
        ex = jnp.exp(gx)
        gl = gx[CHUNK - 1:CHUNK, :]
        kb = k * bx
        st[u] = dict(gx=gx, k=k, kb=kb, q=q, egl=jnp.exp(gl), kd=(k * jnp.exp(gl - gx)).astype(BF16),
                     qd=q * ex, rhs=jnp.concatenate([v * bx, kb * ex], axis=1).astype(BF16))

    for u in units:
        rt = None
        for p in _split2(st[u]["gx"]):
            t = _dot_nt(onehot0, jnp.concatenate([p, p], axis=0))
            rt = t if rt is None else rt + t
        st[u]["rt"] = rt
    for u in units:
        s = st[u]
        kbf = s["k"].astype(BF16)
        s["qk"] = _dot_nt(jnp.concatenate([s["kb"], s["q"]], axis=0).astype(BF16),
                          jnp.concatenate([kbf, kbf], axis=0))
    for u in units:
        s = st[u]
        dec = jnp.where(incl, jnp.exp(jnp.where(incl, s["gx"] - s["rt"], 0.0)), 0.0)
        lmat = jnp.where(strict, s["qk"][:CHUNK] * dec, 0.0)
        s["attn"] = jnp.where(incl & (lane < CHUNK), s["qk"][CHUNK:] * dec, 0.0).astype(BF16)
        s["w"] = jnp.where(right, eye_right, -lmat)
    for _ in range(6):
        for u in units:
            s = st[u]
            wb = s["w"].astype(BF16)
            s["prod"] = _dot(wb, jnp.concatenate([wb, zrow_b], axis=0))
        for u in units:
            s = st[u]
            s["w"] = s["prod"] + jnp.where(right, s["w"], 0.0)
    for u in units:
        s = st[u]
        rhs = jnp.concatenate([jnp.zeros_like(s["rhs"]), s["rhs"]], axis=0)
        s["tu"] = _dot(s["w"].astype(BF16), rhs)

    state = [s_ref[h] for h in range(hg)]
    for c in range(cpb):
        nvalid = min(max(t_valid - c * CHUNK, 0), CHUNK)
        ws = []
        for h in range(hg):
            s = st[(c, h)]
            lhs = jnp.concatenate([s["tu"][:, HEAD_DIM:], s["qd"]], axis=0).astype(BF16)
            ws.append(_dot(lhs, state[h].astype(BF16)))
        vnb = [(st[(c, h)]["tu"][:, :HEAD_DIM] - ws[h][:CHUNK]).astype(BF16) for h in range(hg)]
        outs = []
        for h in range(hg):
            s = st[(c, h)]
            outs.append(ws[h][CHUNK:] + _dot(s["attn"], jnp.concatenate([vnb[h], zrow_b], axis=0)))
            state[h] = state[h] * s["egl"] + _dot_tn(s["kd"], vnb[h])
        if nvalid > 0:
            for h in range(hg):
                ov = outs[h][:nvalid]
                on = (ov * lax.rsqrt(jnp.mean(ov * ov, axis=-1, keepdims=True) + EPS)) * gn_ref[...]
                orow = slice(c * CHUNK, c * CHUNK + nvalid)
                o_ref[orow, csl(h)] = (on * _silu(az_ref[orow, csl(h)])).astype(o_ref.dtype)
    for h in range(hg):
        s_ref[h] = state[h]

    @pl.when(i == pl.num_programs(2) - 1)
    def _():
        sout_ref[0] = s_ref[...]


def _gdn(qkv, bx, gx, proj, s0, gn_row, *, b, t_pad, t_valid, nh, az_col0):
    gw = nh * HEAD_DIM
    hg = _pick(nh, GDN_HEADS_PER_STEP)
    ng = nh // hg
    if t_valid < t_pad:
        cpb = t_pad // CHUNK
    else:
        cpb = _pick(t_pad // CHUNK, GDN_CHUNKS_PER_STEP)
    rb = cpb * CHUNK
    nblk = t_pad // rb
    rv = min(rb, t_valid)
    assert nblk == 1 or rv == rb
    hw = hg * HEAD_DIM
    azb = az_col0 // hw
    assert az_col0 % hw == 0
    kern = functools.partial(_gdn_kernel, hg=hg, cpb=cpb, t_valid=rv)
    qspec = lambda off: pl.BlockSpec((rb, hw), lambda bi, g, i, off=off: (bi * nblk + i, off + g))
    return pl.pallas_call(
        kern,
        grid=(b, ng, nblk),
        in_specs=[
            qspec(0), qspec(ng), qspec(2 * ng),
            pl.BlockSpec((rb, hw), lambda bi, g, i: (bi * nblk + i, g)),
            pl.BlockSpec((rb, hw), lambda bi, g, i: (bi * nblk + i, g)),
            pl.BlockSpec((rv, hw), lambda bi, g, i: (bi * nblk + i, azb + g)),
            pl.BlockSpec((1, hg, HEAD_DIM, HEAD_DIM), lambda bi, g, i: (bi, g, 0, 0)),
            pl.BlockSpec((1, HEAD_DIM), lambda bi, g, i: (0, 0)),
        ],
        out_specs=[
            pl.BlockSpec((rv, hw), lambda bi, g, i: (bi * nblk + i, g)),
            pl.BlockSpec((1, hg, HEAD_DIM, HEAD_DIM), lambda bi, g, i: (bi, g, 0, 0)),
        ],
        out_shape=[jax.ShapeDtypeStruct((b * nblk * rv, gw), BF16),
                   jax.ShapeDtypeStruct((b, nh, HEAD_DIM, HEAD_DIM), F32)],
        scratch_shapes=[pltpu.VMEM((hg, HEAD_DIM, HEAD_DIM), F32)],
        compiler_params=_cparams(("parallel", "parallel", "arbitrary")),
        name="gdn",
    )(qkv, qkv, qkv, bx, gx, proj, s0, gn_row)


SB_Z2_SCALE = (HEAD_DIM ** -0.5) * 1.4426950408889634

def _neg_abs(x):
    bits = lax.bitcast_convert_type(x, jnp.uint32) | jnp.uint32(0x80000000)
    return lax.bitcast_convert_type(bits, F32)


def _lane_tile(x, n):
    if n <= HEAD_DIM:
        return x[:, :n]
    return jnp.concatenate([x] * (n // HEAD_DIM), axis=1)


def _sb_blocks(q_bfs, k_blks, v_blks, tri, drops, mask):
    z2s = [_dot_nt(q, k.astype(BF16)) for q, k in zip(q_bfs, k_blks)]
    lss, nlks = [], []
    for z2 in z2s:
        ls = jnp.minimum(z2, 0.0) - jnp.log2(1.0 + jnp.exp2(_neg_abs(z2)))
        lss.append(ls)
        nlk = z2 - ls
        if mask is not None:
            nlk = jnp.where(mask, nlk, 0.0)
        nlks.append(nlk)
    betw = [_dot(nlk.astype(BF16), tri) for nlk in nlks]
    pvs = []
    for ls, bt, drop, v in zip(lss, betw, drops, v_blks):
        a = jnp.exp2(ls - bt - _lane_tile(drop, bt.shape[1]))
        if mask is not None:
            a = jnp.where(mask, a, 0.0)
        pvs.append(_dot(a.astype(BF16), v.astype(BF16)))
    return pvs, [jnp.sum(nlk, axis=-1, keepdims=True) for nlk in nlks]


def _sb_sweep(q_ref, z_ref, o_ref, acc_ref, drop_ref, first, blocks, nblocks, tri, *, hb):
    hsl = lambda h: slice(h * HEAD_DIM, (h + 1) * HEAD_DIM)
    q_bfs = [(q_ref[:, hsl(h)] * SB_Z2_SCALE).astype(BF16) for h in range(hb)]
    rows = q_ref.shape[0]
    k0, v0, tri0, mask0 = first
    pvs, sums = _sb_blocks(q_bfs, k0, v0, tri0, [jnp.zeros((rows, HEAD_DIM), F32)] * hb, mask0)
    for h in range(hb):
        acc_ref[:, hsl(h)] = pvs[h]
        drop_ref[:, hsl(h)] = jnp.broadcast_to(sums[h], (rows, HEAD_DIM))

    def body(step, carry):
        ks, vs = blocks(step)
        drops = [drop_ref[:, hsl(h)] for h in range(hb)]
        pvs, sums = _sb_blocks(q_bfs, ks, vs, tri, drops, None)
        for h in range(hb):
            acc_ref[:, hsl(h)] += pvs[h]
            drop_ref[:, hsl(h)] += jnp.broadcast_to(sums[h], (rows, HEAD_DIM))
        return carry

    lax.fori_loop(0, nblocks, body, 0)
    o_ref[...] = (acc_ref[...] * _silu(z_ref[...])).astype(o_ref.dtype)


def _sb_prompt_kernel(q_ref, k_ref, v_ref, z_ref, tri_ref, o_ref, acc_ref, drop_ref, *, tq, hb):
    qi = pl.program_id(2)
    hsl = lambda h: slice(h * HEAD_DIM, (h + 1) * HEAD_DIM)
    tri = tri_ref[...]
    row = lax.broadcasted_iota(jnp.int32, (tq, tq), 0)
    col = lax.broadcasted_iota(jnp.int32, (tq, tq), 1)

    def load(r0):
        return ([k_ref[pl.ds(r0, tq), hsl(h)] for h in range(hb)],
                [v_ref[pl.ds(r0, tq), hsl(h)] for h in range(hb)])

    k0, v0 = load(pl.multiple_of(qi * tq, tq))
    blocks = lambda step: load(pl.multiple_of((qi - 1 - step) * tq, tq))
    _sb_sweep(q_ref, z_ref, o_ref, acc_ref, drop_ref, (k0, v0, tri, col < row), blocks, qi, tri, hb=hb)


def _sb_heads_per_step(nh, q_col0):
    for hb in SB_HEADS_PER_STEP:
        if nh % hb == 0 and (q_col0 // HEAD_DIM) % hb == 0:
            return hb
    return 1


def _sb_prompt(proj, tri, *, b, t, nh, q_col0):
    sw = nh * HEAD_DIM
    tq = tri.shape[0]
    nq = t // tq
    hb = _sb_heads_per_step(nh, q_col0)
    hw = hb * HEAD_DIM
    ng = nh // hb
    qb = q_col0 // hw
    kern = functools.partial(_sb_prompt_kernel, tq=tq, hb=hb)

    def rows(off):
        return pl.BlockSpec((tq, hw), lambda bi, g, i: (bi * nq + i, qb + off * ng + g))

    def seq(off):
        return pl.BlockSpec((t, hw), lambda bi, g, i: (bi, qb + off * ng + g), pipeline_mode=pl.Buffered(1))

    return pl.pallas_call(
        kern,
        grid=(b, ng, nq),
        in_specs=[rows(0), seq(1), seq(2), rows(3), pl.BlockSpec((tq, tq), lambda bi, g, i: (0, 0))],
        out_specs=pl.BlockSpec((tq, hw), lambda bi, g, i: (bi * nq + i, g)),
        out_shape=jax.ShapeDtypeStruct((b * t, sw), BF16),
        scratch_shapes=[pltpu.VMEM((tq, hw), F32), pltpu.VMEM((tq, hw), F32)],
        compiler_params=_cparams(("parallel", "parallel", "arbitrary")),
        name="sb_prompt",
    )(proj, proj, proj, proj, tri)


def _sb_sample_kernel(q_ref, k_ref, v_ref, z_ref, kp_ref, vp_ref, trin_ref, tri_ref, o_ref,
                      acc_ref, drop_ref, *, t, tk, npast, hb):
    hsl = lambda h: slice(h * HEAD_DIM, (h + 1) * HEAD_DIM)
    row = lax.broadcasted_iota(jnp.int32, (t, t), 0)
    col = lax.broadcasted_iota(jnp.int32, (t, t), 1)
    k0 = [k_ref[:, hsl(h)] for h in range(hb)]
    v0 = [v_ref[:, hsl(h)] for h in range(hb)]

    def blocks(step):
        r0 = pl.multiple_of((npast - 1 - step) * tk, tk)
        return ([kp_ref[h, pl.ds(r0, tk), :] for h in range(hb)],
                [vp_ref[h, pl.ds(r0, tk), :] for h in range(hb)])

    _sb_sweep(q_ref, z_ref, o_ref, acc_ref, drop_ref, (k0, v0, trin_ref[...], col < row), blocks, npast,
              tri_ref[...], hb=hb)


def _sb_sample(proj, k_past, v_past, layer, tri_new, tri, *, b, t, nh, q_col0):
    sw = nh * HEAD_DIM
    past = k_past.shape[3]
    tk = tri.shape[0]
    assert past % tk == 0
    hb = _sb_heads_per_step(nh, q_col0)
    hw = hb * HEAD_DIM
    ng = nh // hb
    qb = q_col0 // hw
    kern = functools.partial(_sb_sample_kernel, t=t, tk=tk, npast=past // tk, hb=hb)

    def cur(off):
        return pl.BlockSpec((t, hw), lambda bi, g: (bi, qb + off * ng + g))

    old = pl.BlockSpec((None, None, hb, past, HEAD_DIM), lambda bi, g: (layer, bi, g, 0, 0))
    return pl.pallas_call(
        kern,
        grid=(b, ng),
        in_specs=[
            cur(0), cur(1), cur(2), cur(3), old, old,
            pl.BlockSpec((t, t), lambda bi, g: (0, 0)),
            pl.BlockSpec((tk, tk), lambda bi, g: (0, 0)),
        ],
        out_specs=pl.BlockSpec((t, hw), lambda bi, g: (bi, g)),
        out_shape=jax.ShapeDtypeStruct((b * t, sw), BF16),
        scratch_shapes=[pltpu.VMEM((t, hw), F32), pltpu.VMEM((t, hw), F32)],
        compiler_params=_cparams(("parallel", "parallel")),
        name="sb_sample",
    )(proj, proj, proj, proj, k_past, v_past, tri_new, tri)


def _sc_kernel(cb_ref, cc_ref, ch_ref, cz_ref, cw_ref, pre_ref, o_ref, tail_ref, us_ref, *, tt, kw):
    i = pl.program_id(1)
    hist = kw - 1
    base = 8

    @pl.when(i == 0)
    def _():
        us_ref[base - hist:base, :] = pre_ref[0]

    us_ref[base:base + tt, :] = cc_ref[...] * ch_ref[...]
    y = us_ref[base - hist:base - hist + tt, :] * cw_ref[0:1, :]
    for t in range(1, kw):
        y = y + us_ref[base - hist + t:base - hist + t + tt, :] * cw_ref[t:t + 1, :]
    o_ref[...] = ((cb_ref[...] * y) * _silu(cz_ref[...])).astype(o_ref.dtype)
    tail = us_ref[base + tt - hist:base + tt, :]
    us_ref[base - hist:base, :] = tail

    @pl.when(i == pl.num_programs(1) - 1)
    def _():
        tail_ref[0] = tail


def _sc(proj, conv_w, prefix, *, b, t, cw, c_col0):
    kw = conv_w.shape[0]
    tt = _pick(t, (256, 128, 64, 32, 16, 8))
    nt = t // tt
    cb = c_col0 // cw
    assert c_col0 % cw == 0
    kern = functools.partial(_sc_kernel, tt=tt, kw=kw)
    col = lambda off: pl.BlockSpec((tt, cw), lambda bi, i, off=off: (bi * nt + i, cb + off))
    return pl.pallas_call(
        kern,
        grid=(b, nt),
        in_specs=[
            col(0), col(1), col(2), col(3),
            pl.BlockSpec((kw, cw), lambda bi, i: (0, 0)),
            pl.BlockSpec((1, kw - 1, cw), lambda bi, i: (bi, 0, 0)),
        ],
        out_specs=[
            pl.BlockSpec((tt, cw), lambda bi, i: (bi * nt + i, 0)),
            pl.BlockSpec((1, kw - 1, cw), lambda bi, i: (bi, 0, 0)),
        ],
        out_shape=[jax.ShapeDtypeStruct((b * t, cw), BF16),
                   jax.ShapeDtypeStruct((b, kw - 1, cw), F32)],
        scratch_shapes=[pltpu.VMEM((tt + 8, cw), F32)],
        compiler_params=_cparams(("parallel", "arbitrary")),
        name="sc",
    )(proj, proj, proj, proj, conv_w, prefix)


def _outproj_kernel(a_ref, b_ref, c_ref, wa_ref, wb_ref, wc_ref, x_ref, g_ref, o_ref):
    y = _dot(a_ref[...], wa_ref[...]) + _dot(b_ref[...], wb_ref[...]) + _dot(c_ref[...], wc_ref[...])
    ms = jnp.mean(y * y, axis=-1, keepdims=True)
    o_ref[...] = x_ref[...] + (y * lax.rsqrt(ms + EPS)) * g_ref[...]


def _outproj(oa, ob, oc, w_out, layer, x2d, g_row):
    m = oa.shape[0]
    d = w_out.shape[2]
    ka, kb, kc = oa.shape[1], ob.shape[1], oc.shape[1]
    assert ka % kb == 0 and (ka + kb) % kc == 0
    tm = _pick(m, OUTPROJ_ROWS)
    kern = _outproj_kernel

    def wspec(rows, blk):
        return pl.BlockSpec((None, rows, d), lambda i: (layer, blk, 0), pipeline_mode=pl.Buffered(1))

    return pl.pallas_call(
        kern,
        grid=(m // tm,),
        in_specs=[
            pl.BlockSpec((tm, ka), lambda i: (i, 0)),
            pl.BlockSpec((tm, kb), lambda i: (i, 0)),
            pl.BlockSpec((tm, kc), lambda i: (i, 0)),
            wspec(ka, 0), wspec(kb, ka // kb), wspec(kc, (ka + kb) // kc),
            pl.BlockSpec((tm, d), lambda i: (i, 0)),
            pl.BlockSpec((1, d), lambda i: (0, 0)),
        ],
        out_specs=pl.BlockSpec((tm, d), lambda i: (i, 0)),
        out_shape=jax.ShapeDtypeStruct((m, d), F32),
        compiler_params=_cparams(("parallel",)),
        name="outproj",
    )(oa, ob, oc, w_out, w_out, w_out, x2d, g_row)


def _layer(x, lw, consts, *, nh_g, nh_s, cw, sb_past, gdn_s0, gdn_conv0, sc_conv0):
    b, t, d = x.shape
    gw, sw = nh_g * HEAD_DIM, nh_s * HEAD_DIM
    x2d = x.reshape(b * t, d)
    proj, og = _proj(x2d, lw["g_pre"], lw["w_main"], lw["w_gate"], lw["layer"])

    t_pad = -(-t // CHUNK) * CHUNK
    qkv, bx, gx = _gdn_pre(proj, og, lw["gdn_conv_w"], gdn_conv0, lw["alog"], lw["dtb"],
                           consts["expand"], consts["linc"], b=b, t=t, t_out=t_pad, nh=nh_g)
    o_a, s_new = _gdn(qkv, bx, gx, proj, gdn_s0, lw["gn"], b=b, t_pad=t_pad, t_valid=t,
                      nh=nh_g, az_col0=3 * gw)

    q_col0 = 4 * gw
    if sb_past is None:
        o_b = _sb_prompt(proj, consts["tri"], b=b, t=t, nh=nh_s, q_col0=q_col0)
    else:
        o_b = _sb_sample(proj, sb_past[0], sb_past[1], sb_past[2], consts["tri_new"], consts["tri"],
                         b=b, t=t, nh=nh_s, q_col0=q_col0)

    c_col0 = 4 * gw + 4 * sw
    o_c, sc_tail = _sc(proj, lw["sc_conv_w"], sc_conv0, b=b, t=t, cw=cw, c_col0=c_col0)

    x_new = _outproj(o_a, o_b, o_c, lw["w_out"], lw["layer"], x2d, lw["g_post"]).reshape(b, t, d)

    kcol = q_col0 + sw
    proj3 = proj.reshape(b, t, proj.shape[1])
    kb = proj3[:, :, kcol:kcol + sw].reshape(b, t, nh_s, HEAD_DIM)
    vb = proj3[:, :, kcol + sw:kcol + 2 * sw].reshape(b, t, nh_s, HEAD_DIM)
    kw = lw["gdn_conv_w"].shape[0]
    conv_new = proj3[:, t - (kw - 1):, :3 * gw]
    return x_new, (kb, vb, s_new, conv_new, sc_tail)


def _tri(n):
    r = jnp.arange(n)
    return (r[:, None] > r[None, :]).astype(BF16)


def kernel(x_prompt, x_sample, cache_sb_k, cache_sb_v, state_gdn, state_gdn_conv, state_sc_conv,
           w_in, w_out, norm_pre, norm_post, gdn_conv_w, gdn_a_log, gdn_dt_bias, gdn_norm, sc_conv_w):
    depth, d, _ = w_in.shape
    nh_g = gdn_a_log.shape[1]
    nh_s = cache_sb_k.shape[3]
    cw = sc_conv_w.shape[2]
    gw, sw = nh_g * HEAD_DIM, nh_s * HEAD_DIM
    bp, tp, _ = x_prompt.shape
    bs, ts, _ = x_sample.shape
    past = cache_sb_k.shape[2]
    kw_g = gdn_conv_w.shape[1]
    kw_s = sc_conv_w.shape[1]
    assert nh_g <= GATE_LANES and tp >= kw_g and ts >= kw_g and tp % CHUNK == 0

    w_main, w_gate = _repack(jnp.swapaxes(w_in, 1, 2), g0=4 * gw, nh=nh_g)
    w_gate = w_gate.astype(BF16)
    w_out_b16 = w_out.astype(BF16)
    lane_pad = lambda a: jnp.pad(a.astype(F32), ((0, 0), (0, GATE_LANES - nh_g)))[:, None, :]
    alog = lane_pad(gdn_a_log)
    dtb = lane_pad(gdn_dt_bias)

    lane_head = jnp.arange(gw) // HEAD_DIM
    tq = _pick(tp, (256, 128, 64, 32, 16, 8))
    tk = _pick(past, (256, 128, 64, 32, 16, 8))
    r = jnp.arange(CHUNK)
    consts = {
        "expand": (jnp.arange(GATE_LANES)[:, None] == lane_head[None, :]).astype(BF16),
        "linc": (r[:, None] >= r[None, :]).astype(BF16),
        "tri": _tri(tq),
    }
    consts_s = dict(consts, tri=_tri(tk), tri_new=_tri(ts))

    s0_p = jnp.zeros((bp, nh_g, HEAD_DIM, HEAD_DIM), F32)
    gconv0_p = jnp.zeros((bp, kw_g - 1, 3 * gw), F32)
    sconv0_p = jnp.zeros((bp, kw_s - 1, cw), F32)

    kpast = jnp.transpose(cache_sb_k, (0, 1, 3, 2, 4)).astype(BF16)
    vpast = jnp.transpose(cache_sb_v, (0, 1, 3, 2, 4)).astype(BF16)

    yp, ys = x_prompt, x_sample
    new_p = ([], [], [], [], [])
    new_s = ([], [], [], [], [])
    for l in range(depth):
        lw = {
            "layer": l, "g_pre": norm_pre[l][None, :], "g_post": norm_post[l][None, :],
            "w_main": w_main, "w_gate": w_gate, "w_out": w_out_b16,
            "gdn_conv_w": gdn_conv_w[l], "alog": alog[l], "dtb": dtb[l], "gn": gdn_norm[l][None, :],
            "sc_conv_w": sc_conv_w[l],
        }
        yp, st_p = _layer(yp, lw, consts, nh_g=nh_g, nh_s=nh_s, cw=cw, sb_past=None,
                          gdn_s0=s0_p, gdn_conv0=gconv0_p, sc_conv0=sconv0_p)
        ys, st_s = _layer(ys, lw, consts_s, nh_g=nh_g, nh_s=nh_s, cw=cw, sb_past=(kpast, vpast, l),
                          gdn_s0=state_gdn[l], gdn_conv0=state_gdn_conv[l], sc_conv0=state_sc_conv[l])
        for i in range(5):
            new_p[i].append(st_p[i])
            new_s[i].append(st_s[i])
    outs_p = [jnp.stack(a, axis=0) for a in new_p]
    outs_s = [jnp.stack(a, axis=0) for a in new_s]
    return (yp, ys, *outs_p, *outs_s)
```

```python
import functools

import jax
import jax.numpy as jnp
from jax import lax
from jax.experimental import pallas as pl
from jax.experimental.pallas import tpu as pltpu

F32 = jnp.float32
BF16 = jnp.bfloat16

HEAD_DIM = 128
CHUNK = 64
EPS = 1e-6
GATE_LANES = 128
VMEM_LIMIT = 56 * 1024 * 1024
OUTPROJ_ROWS = (128, 64, 32, 16)
GDN_HEADS_PER_STEP = (12, 6, 4, 3, 2, 1)
GDN_CHUNKS_PER_STEP = (2, 1)
SB_HEADS_PER_STEP = (6, 4, 3, 2, 1)


def _cparams(sem, vmem=VMEM_LIMIT):
    return pltpu.CompilerParams(dimension_semantics=sem, vmem_limit_bytes=vmem)


def _pick(n, prefs):
    for p in prefs:
        if n % p == 0:
            return p
    return n


def _dot(a, b):
    return jnp.dot(a, b, preferred_element_type=F32)


def _dot_nt(a, b):
    return lax.dot_general(a, b, (((1,), (1,)), ((), ())), preferred_element_type=F32)


def _dot_tn(a, b):
    return lax.dot_general(a, b, (((0,), (0,)), ((), ())), preferred_element_type=F32)


def _split2(x):
    hi = x.astype(BF16)
    lo = (x - hi.astype(F32)).astype(BF16)
    return hi, lo


def _split3(x):
    hi = x.astype(BF16)
    r = x - hi.astype(F32)
    mid = r.astype(BF16)
    lo = (r - mid.astype(F32)).astype(BF16)
    return hi, mid, lo


def _sigmoid(x):
    return 1.0 / (1.0 + jnp.exp(-x))


def _silu(x):
    return x * _sigmoid(x)


def _softplus(x):
    return jnp.maximum(x, 0.0) + jnp.log1p(jnp.exp(-jnp.abs(x)))


def _repack_kernel(w_ref, wm_ref, wg_ref, scr_ref, *, gi, nh, tr):
    i = pl.program_id(1)
    s = 2 * nh

    @pl.when(i < gi)
    def _():
        wm_ref[...] = w_ref[...].astype(BF16)

    @pl.when(i == gi)
    def _():
        wg_ref[...] = jnp.zeros(wg_ref.shape, F32)
        wg_ref[0:nh, :] = w_ref[0:nh, :]
        wg_ref[GATE_LANES:GATE_LANES + nh, :] = w_ref[nh:s, :]

    @pl.when(i > gi)
    def _():
        scr_ref[tr - s:tr, :] = w_ref[0:s, :]
        wm_ref[...] = scr_ref[...].astype(BF16)

    @pl.when((i >= gi) & (i < pl.num_programs(1) - 1))
    def _():
        scr_ref[0:tr - s, :] = w_ref[s:tr, :]


def _repack(w_in_t, *, g0, nh):
    depth, d_in, d = w_in_t.shape
    n = d_in - 2 * nh
    tr = next(t for t in (512, 256, 128, 64, 32, 16, 8) if g0 % t == 0 and n % t == 0 and t > 2 * nh)
    gi = g0 // tr
    kern = functools.partial(_repack_kernel, gi=gi, nh=nh, tr=tr)
    return pl.pallas_call(
        kern,
        grid=(depth, n // tr + 1),
        in_specs=[pl.BlockSpec((None, tr, d), lambda l, i: (l, i, 0))],
        out_specs=[pl.BlockSpec((None, tr, d), lambda l, i: (l, jnp.where(i < gi, i, i - 1), 0)),
                   pl.BlockSpec((None, 2 * GATE_LANES, d), lambda l, i: (l, 0, 0))],
        out_shape=[jax.ShapeDtypeStruct((depth, n, d), BF16),
                   jax.ShapeDtypeStruct((depth, 2 * GATE_LANES, d), F32)],
        scratch_shapes=[pltpu.VMEM((tr, d), F32)],
        compiler_params=_cparams(("parallel", "arbitrary")),
        name="repack",
    )(w_in_t)


def _norm_kernel(x_ref, g_ref, h_ref):
    x = x_ref[...]
    ms = jnp.mean(x * x, axis=-1, keepdims=True)
    h_ref[...] = ((x * lax.rsqrt(ms + EPS)) * g_ref[...]).astype(h_ref.dtype)


def _norm(x2d, g_row):
    m, d = x2d.shape
    tm = _pick(m, (256, 128, 64, 32, 16, 8))
    return pl.pallas_call(
        _norm_kernel,
        grid=(m // tm,),
        in_specs=[pl.BlockSpec((tm, d), lambda i: (i, 0)), pl.BlockSpec((1, d), lambda i: (0, 0))],
        out_specs=pl.BlockSpec((tm, d), lambda i: (i, 0)),
        out_shape=jax.ShapeDtypeStruct((m, d), BF16),
        compiler_params=_cparams(("parallel",)),
        name="norm",
    )(x2d, g_row)


def _proj_kernel(h_ref, w_ref, wg_ref, o_ref, og_ref):
    @pl.when(pl.program_id(1) == 0)
    def _():
        og_ref[...] = _dot_nt(h_ref[...], wg_ref[...])

    o_ref[...] = _dot_nt(h_ref[...], w_ref[...])


def _proj(h2d, w_main, w_gate, layer):
    m, d = h2d.shape
    n = w_main.shape[1]
    tm = _pick(m, (1024, 512, 256, 128, 64, 32, 16))
    tn = _pick(n, (1024, 512, 256, 128))
    return pl.pallas_call(
        _proj_kernel,
        grid=(m // tm, n // tn),
        in_specs=[
            pl.BlockSpec((tm, d), lambda i, j: (i, 0)),
            pl.BlockSpec((None, tn, d), lambda i, j: (layer, j, 0)),
            pl.BlockSpec((None, 2 * GATE_LANES, d), lambda i, j: (layer, 0, 0)),
        ],
        out_specs=[
            pl.BlockSpec((tm, tn), lambda i, j: (i, j)),
            pl.BlockSpec((tm, 2 * GATE_LANES), lambda i, j: (i, 0)),
        ],
        out_shape=[jax.ShapeDtypeStruct((m, n), F32),
                   jax.ShapeDtypeStruct((m, 2 * GATE_LANES), F32)],
        compiler_params=_cparams(("parallel", "arbitrary")),
        name="proj",
    )(h2d, w_main, w_gate)


def _gdn_pre_kernel(x_ref, og_ref, cw_ref, pre_ref, alog_ref, dtb_ref, e_ref, linc_ref,
                    qkv_ref, bx_ref, gx_ref, xs_ref, *, tt, tt_out, nh, kw):
    i = pl.program_id(1)
    hist = kw - 1
    base = 8

    @pl.when(i == 0)
    def _():
        xs_ref[base - hist:base, :] = pre_ref[0]

    xs_ref[base:base + tt, :] = x_ref[...]
    for c in range(3 * nh):
        cs = slice(c * HEAD_DIM, (c + 1) * HEAD_DIM)
        y = xs_ref[base - hist:base - hist + tt, cs] * cw_ref[0:1, cs]
        for t in range(1, kw):
            y = y + xs_ref[base - hist + t:base - hist + t + tt, cs] * cw_ref[t:t + 1, cs]
        y = _silu(y)
        if c < 2 * nh:
            y = y * lax.rsqrt(jnp.sum(y * y, axis=-1, keepdims=True) + EPS)
        if c < nh:
            y = y * (HEAD_DIM ** -0.5)
        qkv_ref[0:tt, cs] = y.astype(qkv_ref.dtype)
    if tt_out > tt:
        qkv_ref[tt:tt_out, :] = jnp.zeros((tt_out - tt, qkv_ref.shape[1]), qkv_ref.dtype)
    xs_ref[base - hist:base, :] = xs_ref[base + tt - hist:base + tt, :]

    og = og_ref[...]
    g = -jnp.exp(alog_ref[...]) * _softplus(og[:, :GATE_LANES] + dtb_ref[...])
    beta = _sigmoid(og[:, GATE_LANES:])
    if tt_out > tt:
        pad = jnp.zeros((tt_out - tt, GATE_LANES), F32)
        g = jnp.concatenate([g, pad], axis=0)
        beta = jnp.concatenate([beta, pad], axis=0)
    e = e_ref[...]
    linc = linc_ref[...]
    bx = None
    for p in _split2(beta):
        t = _dot(p, e)
        bx = t if bx is None else bx + t
    bx_ref[...] = bx
    for r in range(tt_out // CHUNK):
        rs = slice(r * CHUNK, (r + 1) * CHUNK)
        gc = None
        for p in _split3(g[rs]):
            t = _dot(linc, p)
            gc = t if gc is None else gc + t
        gcx = None
        for p in _split2(gc):
            t = _dot(p, e)
            gcx = t if gcx is None else gcx + t
        gx_ref[rs, :] = gcx


def _gdn_pre(proj, og, conv_w, prefix, alog_row, dtb_row, expand, linc, *, b, t, t_out, nh):
    gw = nh * HEAD_DIM
    kw = conv_w.shape[0]
    if t_out > t:
        tt, tt_out = t, t_out
    else:
        tt = _pick(t, (256, 128, 64))
        tt_out = tt
    nt = t // tt
    kern = functools.partial(_gdn_pre_kernel, tt=tt, tt_out=tt_out, nh=nh, kw=kw)
    return pl.pallas_call(
        kern,
        grid=(b, nt),
        in_specs=[
            pl.BlockSpec((tt, 3 * gw), lambda bi, i: (bi * nt + i, 0)),
            pl.BlockSpec((tt, 2 * GATE_LANES), lambda bi, i: (bi * nt + i, 0)),
            pl.BlockSpec((kw, 3 * gw), lambda bi, i: (0, 0)),
            pl.BlockSpec((1, kw - 1, 3 * gw), lambda bi, i: (bi, 0, 0)),
            pl.BlockSpec((1, GATE_LANES), lambda bi, i: (0, 0)),
            pl.BlockSpec((1, GATE_LANES), lambda bi, i: (0, 0)),
            pl.BlockSpec((GATE_LANES, gw), lambda bi, i: (0, 0)),
            pl.BlockSpec((CHUNK, CHUNK), lambda bi, i: (0, 0)),
        ],
        out_specs=[
            pl.BlockSpec((tt_out, 3 * gw), lambda bi, i: (bi * nt + i, 0)),
            pl.BlockSpec((tt_out, gw), lambda bi, i: (bi * nt + i, 0)),
            pl.BlockSpec((tt_out, gw), lambda bi, i: (bi * nt + i, 0)),
        ],
        out_shape=[jax.ShapeDtypeStruct((b * nt * tt_out, 3 * gw), F32),
                   jax.ShapeDtypeStruct((b * nt * tt_out, gw), F32),
                   jax.ShapeDtypeStruct((b * nt * tt_out, gw), F32)],
        scratch_shapes=[pltpu.VMEM((tt + 8, 3 * gw), F32)],
        compiler_params=_cparams(("parallel", "arbitrary")),
        name="gdn_pre",
    )(proj, og, conv_w, prefix, alog_row, dtb_row, expand, linc)


def _gdn_kernel(q_ref, k_ref, v_ref, bx_ref, gx_ref, az_ref, s0_ref, gn_ref,
                o_ref, sout_ref, s_ref, *, hg, cpb, t_valid):
    i = pl.program_id(2)
    c2 = 2 * CHUNK

    @pl.when(i == 0)
    def _():
        s_ref[...] = s0_ref[0]

    row = lax.broadcasted_iota(jnp.int32, (CHUNK, c2), 0)
    lane = lax.broadcasted_iota(jnp.int32, (CHUNK, c2), 1)
    col = jnp.where(lane >= CHUNK, lane - CHUNK, lane)
    incl = row >= col
    strict = row > col
    right = lane >= CHUNK
    eye_right = jnp.where(right & (row == col), 1.0, 0.0).astype(F32)
    onehot0 = jnp.where(lane == 0, 1.0, 0.0).astype(BF16)
    zrow_b = jnp.zeros((CHUNK, HEAD_DIM), BF16)

    units = [(c, h) for c in range(cpb) for h in range(hg)]
    rsl = lambda c: slice(c * CHUNK, (c + 1) * CHUNK)
    csl = lambda h: slice(h * HEAD_DIM, (h + 1) * HEAD_DIM)
    st = {}
    for u in units:
        c, h = u
        q = q_ref[rsl(c), csl(h)]
        k = k_ref[rsl(c), csl(h)]
        v = v_ref[rsl(c), csl(h)]
        bx = bx_ref[rsl(c), csl(h)]
        gx = gx_ref[rsl(c), csl(h)]
        ex = jnp.exp(gx)
        gl = gx[CHUNK - 1:CHUNK, :]
        kb = k * bx
        st[u] = dict(gx=gx, k=k, kb=kb, q=q, egl=jnp.exp(gl), kd=(k * jnp.exp(gl - gx)).astype(BF16),
                     qd=q * ex, rhs=jnp.concatenate([v * bx, kb * ex], axis=1).astype(BF16))

    for u in units:
        rt = None
        for p in _split2(st[u]["gx"]):
            t = _dot_nt(onehot0, jnp.concatenate([p, p], axis=0))
            rt = t if rt is None else rt + t
        st[u]["rt"] = rt
    for u in units:
        s = st[u]
        kbf = s["k"].astype(BF16)
        s["qk"] = _dot_nt(jnp.concatenate([s["kb"], s["q"]], axis=0).astype(BF16),
                          jnp.concatenate([kbf, kbf], axis=0))
    for u in units:
        s = st[u]
        dec = jnp.where(incl, jnp.exp(jnp.where(incl, s["gx"] - s["rt"], 0.0)), 0.0)
        lmat = jnp.where(strict, s["qk"][:CHUNK] * dec, 0.0)
        s["attn"] = jnp.where(incl & (lane < CHUNK), s["qk"][CHUNK:] * dec, 0.0).astype(BF16)
        s["w"] = jnp.where(right, eye_right, -lmat)
    for _ in range(6):
        for u in units:
            s = st[u]
            wb = s["w"].astype(BF16)
            s["prod"] = _dot(wb, jnp.concatenate([wb, zrow_b], axis=0))
        for u in units:
            s = st[u]
            s["w"] = s["prod"] + jnp.where(right, s["w"], 0.0)
    for u in units:
        s = st[u]
        rhs = jnp.concatenate([jnp.zeros_like(s["rhs"]), s["rhs"]], axis=0)
        s["tu"] = _dot(s["w"].astype(BF16), rhs)

    state = [s_ref[h] for h in range(hg)]
    for c in range(cpb):
        nvalid = min(max(t_valid - c * CHUNK, 0), CHUNK)
        ws = []
        for h in range(hg):
            s = st[(c, h)]
            lhs = jnp.concatenate([s["tu"][:, HEAD_DIM:], s["qd"]], axis=0).astype(BF16)
            ws.append(_dot(lhs, state[h].astype(BF16)))
        vnb = [(st[(c, h)]["tu"][:, :HEAD_DIM] - ws[h][:CHUNK]).astype(BF16) for h in range(hg)]
        outs = []
        for h in range(hg):
            s = st[(c, h)]
            outs.append(ws[h][CHUNK:] + _dot(s["attn"], jnp.concatenate([vnb[h], zrow_b], axis=0)))
            state[h] = state[h] * s["egl"] + _dot_tn(s["kd"], vnb[h])
        if nvalid > 0:
            for h in range(hg):
                ov = outs[h][:nvalid]
                on = (ov * lax.rsqrt(jnp.mean(ov * ov, axis=-1, keepdims=True) + EPS)) * gn_ref[...]
                orow = slice(c * CHUNK, c * CHUNK + nvalid)
                o_ref[orow, csl(h)] = (on * _silu(az_ref[orow, csl(h)])).astype(o_ref.dtype)
    for h in range(hg):
        s_ref[h] = state[h]

    @pl.when(i == pl.num_programs(2) - 1)
    def _():
        sout_ref[0] = s_ref[...]


def _gdn(qkv, bx, gx, proj, s0, gn_row, *, b, t_pad, t_valid, nh, az_col0):
    gw = nh * HEAD_DIM
    hg = _pick(nh, GDN_HEADS_PER_STEP)
    ng = nh // hg
    if t_valid < t_pad:
        cpb = t_pad // CHUNK
    else:
        cpb = _pick(t_pad // CHUNK, GDN_CHUNKS_PER_STEP)
    rb = cpb * CHUNK
    nblk = t_pad // rb
    rv = min(rb, t_valid)
    assert nblk == 1 or rv == rb
    hw = hg * HEAD_DIM
    azb = az_col0 // hw
    assert az_col0 % hw == 0
    kern = functools.partial(_gdn_kernel, hg=hg, cpb=cpb, t_valid=rv)
    qspec = lambda off: pl.BlockSpec((rb, hw), lambda bi, g, i, off=off: (bi * nblk + i, off + g))
    return pl.pallas_call(
        kern,
        grid=(b, ng, nblk),
        in_specs=[
            qspec(0), qspec(ng), qspec(2 * ng),
            pl.BlockSpec((rb, hw), lambda bi, g, i: (bi * nblk + i, g)),
            pl.BlockSpec((rb, hw), lambda bi, g, i: (bi * nblk + i, g)),
            pl.BlockSpec((rv, hw), lambda bi, g, i: (bi * nblk + i, azb + g)),
            pl.BlockSpec((1, hg, HEAD_DIM, HEAD_DIM), lambda bi, g, i: (bi, g, 0, 0)),
            pl.BlockSpec((1, HEAD_DIM), lambda bi, g, i: (0, 0)),
        ],
        out_specs=[
            pl.BlockSpec((rv, hw), lambda bi, g, i: (bi * nblk + i, g)),
            pl.BlockSpec((1, hg, HEAD_DIM, HEAD_DIM), lambda bi, g, i: (bi, g, 0, 0)),
        ],
        out_shape=[jax.ShapeDtypeStruct((b * nblk * rv, gw), BF16),
                   jax.ShapeDtypeStruct((b, nh, HEAD_DIM, HEAD_DIM), F32)],
        scratch_shapes=[pltpu.VMEM((hg, HEAD_DIM, HEAD_DIM), F32)],
        compiler_params=_cparams(("parallel", "parallel", "arbitrary")),
        name="gdn",
    )(qkv, qkv, qkv, bx, gx, proj, s0, gn_row)


SB_Z2_SCALE = (HEAD_DIM ** -0.5) * 1.4426950408889634

def _neg_abs(x):
    bits = lax.bitcast_convert_type(x, jnp.uint32) | jnp.uint32(0x80000000)
    return lax.bitcast_convert_type(bits, F32)


def _lane_tile(x, n):
    if n <= HEAD_DIM:
        return x[:, :n]
    return jnp.concatenate([x] * (n // HEAD_DIM), axis=1)


def _sb_blocks(q_bfs, k_blks, v_blks, tri, drops, mask):
    z2s = [_dot_nt(q, k.astype(BF16)) for q, k in zip(q_bfs, k_blks)]
    lss, nlks = [], []
    for z2 in z2s:
        ls = jnp.minimum(z2, 0.0) - jnp.log2(1.0 + jnp.exp2(_neg_abs(z2)))
        lss.append(ls)
        nlk = z2 - ls
        if mask is not None:
            nlk = jnp.where(mask, nlk, 0.0)
        nlks.append(nlk)
    betw = [_dot(nlk.astype(BF16), tri) for nlk in nlks]
    pvs = []
    for ls, bt, drop, v in zip(lss, betw, drops, v_blks):
        a = jnp.exp2(ls - bt - _lane_tile(drop, bt.shape[1]))
        if mask is not None:
            a = jnp.where(mask, a, 0.0)
        pvs.append(_dot(a.astype(BF16), v.astype(BF16)))
    return pvs, [jnp.sum(nlk, axis=-1, keepdims=True) for nlk in nlks]


def _sb_sweep(q_ref, z_ref, o_ref, acc_ref, drop_ref, first, blocks, nblocks, tri, *, hb):
    hsl = lambda h: slice(h * HEAD_DIM, (h + 1) * HEAD_DIM)
    q_bfs = [(q_ref[:, hsl(h)] * SB_Z2_SCALE).astype(BF16) for h in range(hb)]
    rows = q_ref.shape[0]
    k0, v0, tri0, mask0 = first
    pvs, sums = _sb_blocks(q_bfs, k0, v0, tri0, [jnp.zeros((rows, HEAD_DIM), F32)] * hb, mask0)
    for h in range(hb):
        acc_ref[:, hsl(h)] = pvs[h]
        drop_ref[:, hsl(h)] = jnp.broadcast_to(sums[h], (rows, HEAD_DIM))

    def body(step, carry):
        ks, vs = blocks(step)
        drops = [drop_ref[:, hsl(h)] for h in range(hb)]
        pvs, sums = _sb_blocks(q_bfs, ks, vs, tri, drops, None)
        for h in range(hb):
            acc_ref[:, hsl(h)] += pvs[h]
            drop_ref[:, hsl(h)] += jnp.broadcast_to(sums[h], (rows, HEAD_DIM))
        return carry

    lax.fori_loop(0, nblocks, body, 0)
    o_ref[...] = (acc_ref[...] * _silu(z_ref[...])).astype(o_ref.dtype)


def _sb_prompt_kernel(q_ref, k_ref, v_ref, z_ref, tri_ref, o_ref, acc_ref, drop_ref, *, tq, hb):
    qi = pl.program_id(2)
    hsl = lambda h: slice(h * HEAD_DIM, (h + 1) * HEAD_DIM)
    tri = tri_ref[...]
    row = lax.broadcasted_iota(jnp.int32, (tq, tq), 0)
    col = lax.broadcasted_iota(jnp.int32, (tq, tq), 1)

    def load(r0):
        return ([k_ref[pl.ds(r0, tq), hsl(h)] for h in range(hb)],
                [v_ref[pl.ds(r0, tq), hsl(h)] for h in range(hb)])

    k0, v0 = load(pl.multiple_of(qi * tq, tq))
    blocks = lambda step: load(pl.multiple_of((qi - 1 - step) * tq, tq))
    _sb_sweep(q_ref, z_ref, o_ref, acc_ref, drop_ref, (k0, v0, tri, col < row), blocks, qi, tri, hb=hb)


def _sb_heads_per_step(nh, q_col0):
    for hb in SB_HEADS_PER_STEP:
        if nh % hb == 0 and (q_col0 // HEAD_DIM) % hb == 0:
            return hb
    return 1


def _sb_prompt(proj, tri, *, b, t, nh, q_col0):
    sw = nh * HEAD_DIM
    tq = tri.shape[0]
    nq = t // tq
    hb = _sb_heads_per_step(nh, q_col0)
    hw = hb * HEAD_DIM
    ng = nh // hb
    qb = q_col0 // hw
    kern = functools.partial(_sb_prompt_kernel, tq=tq, hb=hb)

    def rows(off):
        return pl.BlockSpec((tq, hw), lambda bi, g, i: (bi * nq + i, qb + off * ng + g))

    def seq(off):
        return pl.BlockSpec((t, hw), lambda bi, g, i: (bi, qb + off * ng + g), pipeline_mode=pl.Buffered(1))

    return pl.pallas_call(
        kern,
        grid=(b, ng, nq),
        in_specs=[rows(0), seq(1), seq(2), rows(3), pl.BlockSpec((tq, tq), lambda bi, g, i: (0, 0))],
        out_specs=pl.BlockSpec((tq, hw), lambda bi, g, i: (bi * nq + i, g)),
        out_shape=jax.ShapeDtypeStruct((b * t, sw), BF16),
        scratch_shapes=[pltpu.VMEM((tq, hw), F32), pltpu.VMEM((tq, hw), F32)],
        compiler_params=_cparams(("parallel", "parallel", "arbitrary")),
        name="sb_prompt",
    )(proj, proj, proj, proj, tri)


def _sb_sample_kernel(q_ref, k_ref, v_ref, z_ref, kp_ref, vp_ref, trin_ref, tri_ref, o_ref,
                      acc_ref, drop_ref, *, t, tk, npast, hb):
    hsl = lambda h: slice(h * HEAD_DIM, (h + 1) * HEAD_DIM)
    row = lax.broadcasted_iota(jnp.int32, (t, t), 0)
    col = lax.broadcasted_iota(jnp.int32, (t, t), 1)
    k0 = [k_ref[:, hsl(h)] for h in range(hb)]
    v0 = [v_ref[:, hsl(h)] for h in range(hb)]

    def blocks(step):
        r0 = pl.multiple_of((npast - 1 - step) * tk, tk)
        return ([kp_ref[h, pl.ds(r0, tk), :] for h in range(hb)],
                [vp_ref[h, pl.ds(r0, tk), :] for h in range(hb)])

    _sb_sweep(q_ref, z_ref, o_ref, acc_ref, drop_ref, (k0, v0, trin_ref[...], col < row), blocks, npast,
              tri_ref[...], hb=hb)


def _sb_sample(proj, k_past, v_past, layer, tri_new, tri, *, b, t, nh, q_col0):
    sw = nh * HEAD_DIM
    past = k_past.shape[3]
    tk = tri.shape[0]
    assert past % tk == 0
    hb = _sb_heads_per_step(nh, q_col0)
    hw = hb * HEAD_DIM
    ng = nh // hb
    qb = q_col0 // hw
    kern = functools.partial(_sb_sample_kernel, t=t, tk=tk, npast=past // tk, hb=hb)

    def cur(off):
        return pl.BlockSpec((t, hw), lambda bi, g: (bi, qb + off * ng + g))

    old = pl.BlockSpec((None, None, hb, past, HEAD_DIM), lambda bi, g: (layer, bi, g, 0, 0))
    return pl.pallas_call(
        kern,
        grid=(b, ng),
        in_specs=[
            cur(0), cur(1), cur(2), cur(3), old, old,
            pl.BlockSpec((t, t), lambda bi, g: (0, 0)),
            pl.BlockSpec((tk, tk), lambda bi, g: (0, 0)),
        ],
        out_specs=pl.BlockSpec((t, hw), lambda bi, g: (bi, g)),
        out_shape=jax.ShapeDtypeStruct((b * t, sw), BF16),
        scratch_shapes=[pltpu.VMEM((t, hw), F32), pltpu.VMEM((t, hw), F32)],
        compiler_params=_cparams(("parallel", "parallel")),
        name="sb_sample",
    )(proj, proj, proj, proj, k_past, v_past, tri_new, tri)


def _sc_kernel(cb_ref, cc_ref, ch_ref, cz_ref, cw_ref, pre_ref, o_ref, tail_ref, us_ref, *, tt, kw):
    i = pl.program_id(1)
    hist = kw - 1
    base = 8

    @pl.when(i == 0)
    def _():
        us_ref[base - hist:base, :] = pre_ref[0]

    us_ref[base:base + tt, :] = cc_ref[...] * ch_ref[...]
    y = us_ref[base - hist:base - hist + tt, :] * cw_ref[0:1, :]
    for t in range(1, kw):
        y = y + us_ref[base - hist + t:base - hist + t + tt, :] * cw_ref[t:t + 1, :]
    o_ref[...] = ((cb_ref[...] * y) * _silu(cz_ref[...])).astype(o_ref.dtype)
    tail = us_ref[base + tt - hist:base + tt, :]
    us_ref[base - hist:base, :] = tail

    @pl.when(i == pl.num_programs(1) - 1)
    def _():
        tail_ref[0] = tail


def _sc(proj, conv_w, prefix, *, b, t, cw, c_col0):
    kw = conv_w.shape[0]
    tt = _pick(t, (256, 128, 64, 32, 16, 8))
    nt = t // tt
    cb = c_col0 // cw
    assert c_col0 % cw == 0
    kern = functools.partial(_sc_kernel, tt=tt, kw=kw)
    col = lambda off: pl.BlockSpec((tt, cw), lambda bi, i, off=off: (bi * nt + i, cb + off))
    return pl.pallas_call(
        kern,
        grid=(b, nt),
        in_specs=[
            col(0), col(1), col(2), col(3),
            pl.BlockSpec((kw, cw), lambda bi, i: (0, 0)),
            pl.BlockSpec((1, kw - 1, cw), lambda bi, i: (bi, 0, 0)),
        ],
        out_specs=[
            pl.BlockSpec((tt, cw), lambda bi, i: (bi * nt + i, 0)),
            pl.BlockSpec((1, kw - 1, cw), lambda bi, i: (bi, 0, 0)),
        ],
        out_shape=[jax.ShapeDtypeStruct((b * t, cw), BF16),
                   jax.ShapeDtypeStruct((b, kw - 1, cw), F32)],
        scratch_shapes=[pltpu.VMEM((tt + 8, cw), F32)],
        compiler_params=_cparams(("parallel", "arbitrary")),
        name="sc",
    )(proj, proj, proj, proj, conv_w, prefix)


def _outproj_kernel(a_ref, b_ref, c_ref, wa_ref, wb_ref, wc_ref, x_ref, g_ref, gn_ref, o_ref, *h_ref):
    y = _dot(a_ref[...], wa_ref[...]) + _dot(b_ref[...], wb_ref[...]) + _dot(c_ref[...], wc_ref[...])
    ms = jnp.mean(y * y, axis=-1, keepdims=True)
    xn = x_ref[...] + (y * lax.rsqrt(ms + EPS)) * g_ref[...]
    o_ref[...] = xn
    if h_ref:
        msn = jnp.mean(xn * xn, axis=-1, keepdims=True)
        h_ref[0][...] = ((xn * lax.rsqrt(msn + EPS)) * gn_ref[...]).astype(BF16)


def _outproj(oa, ob, oc, w_out, layer, x2d, g_row, g_next):
    m = oa.shape[0]
    d = w_out.shape[2]
    ka, kb, kc = oa.shape[1], ob.shape[1], oc.shape[1]
    assert ka % kb == 0 and (ka + kb) % kc == 0
    tm = _pick(m, OUTPROJ_ROWS)
    emit_h = g_next is not None

    def wspec(rows, blk):
        return pl.BlockSpec((None, rows, d), lambda i: (layer, blk, 0), pipeline_mode=pl.Buffered(1))

    row = pl.BlockSpec((tm, d), lambda i: (i, 0))
    gain = pl.BlockSpec((1, d), lambda i: (0, 0))
    outs = pl.pallas_call(
        _outproj_kernel,
        grid=(m // tm,),
        in_specs=[
            pl.BlockSpec((tm, ka), lambda i: (i, 0)),
            pl.BlockSpec((tm, kb), lambda i: (i, 0)),
            pl.BlockSpec((tm, kc), lambda i: (i, 0)),
            wspec(ka, 0), wspec(kb, ka // kb), wspec(kc, (ka + kb) // kc),
            row, gain, gain,
        ],
        out_specs=[row, row] if emit_h else [row],
        out_shape=([jax.ShapeDtypeStruct((m, d), F32), jax.ShapeDtypeStruct((m, d), BF16)] if emit_h
                   else [jax.ShapeDtypeStruct((m, d), F32)]),
        compiler_params=_cparams(("parallel",)),
        name="outproj",
    )(oa, ob, oc, w_out, w_out, w_out, x2d, g_row, g_next if emit_h else g_row)
    return (outs[0], outs[1]) if emit_h else (outs[0], None)


def _layer(x, h2d, lw, consts, *, nh_g, nh_s, cw, sb_past, gdn_s0, gdn_conv0, sc_conv0):
    b, t, d = x.shape
    gw, sw = nh_g * HEAD_DIM, nh_s * HEAD_DIM
    x2d = x.reshape(b * t, d)
    proj, og = _proj(h2d, lw["w_main"], lw["w_gate"], lw["layer"])

    t_pad = -(-t // CHUNK) * CHUNK
    qkv, bx, gx = _gdn_pre(proj, og, lw["gdn_conv_w"], gdn_conv0, lw["alog"], lw["dtb"],
                           consts["expand"], consts["linc"], b=b, t=t, t_out=t_pad, nh=nh_g)
    o_a, s_new = _gdn(qkv, bx, gx, proj, gdn_s0, lw["gn"], b=b, t_pad=t_pad, t_valid=t,
                      nh=nh_g, az_col0=3 * gw)

    q_col0 = 4 * gw
    if sb_past is None:
        o_b = _sb_prompt(proj, consts["tri"], b=b, t=t, nh=nh_s, q_col0=q_col0)
    else:
        o_b = _sb_sample(proj, sb_past[0], sb_past[1], sb_past[2], consts["tri_new"], consts["tri"],
                         b=b, t=t, nh=nh_s, q_col0=q_col0)

    c_col0 = 4 * gw + 4 * sw
    o_c, sc_tail = _sc(proj, lw["sc_conv_w"], sc_conv0, b=b, t=t, cw=cw, c_col0=c_col0)

    x_new, h_next = _outproj(o_a, o_b, o_c, lw["w_out"], lw["layer"], x2d, lw["g_post"], lw["g_pre_next"])
    x_new = x_new.reshape(b, t, d)

    kcol = q_col0 + sw
    proj3 = proj.reshape(b, t, proj.shape[1])
    kb = proj3[:, :, kcol:kcol + sw].reshape(b, t, nh_s, HEAD_DIM)
    vb = proj3[:, :, kcol + sw:kcol + 2 * sw].reshape(b, t, nh_s, HEAD_DIM)
    kw = lw["gdn_conv_w"].shape[0]
    conv_new = proj3[:, t - (kw - 1):, :3 * gw]
    return x_new, h_next, (kb, vb, s_new, conv_new, sc_tail)


def _tri(n):
    r = jnp.arange(n)
    return (r[:, None] > r[None, :]).astype(BF16)


def kernel(x_prompt, x_sample, cache_sb_k, cache_sb_v, state_gdn, state_gdn_conv, state_sc_conv,
           w_in, w_out, norm_pre, norm_post, gdn_conv_w, gdn_a_log, gdn_dt_bias, gdn_norm, sc_conv_w):
    depth, d, _ = w_in.shape
    nh_g = gdn_a_log.shape[1]
    nh_s = cache_sb_k.shape[3]
    cw = sc_conv_w.shape[2]
    gw, sw = nh_g * HEAD_DIM, nh_s * HEAD_DIM
    bp, tp, _ = x_prompt.shape
    bs, ts, _ = x_sample.shape
    past = cache_sb_k.shape[2]
    kw_g = gdn_conv_w.shape[1]
    kw_s = sc_conv_w.shape[1]
    assert nh_g <= GATE_LANES and tp >= kw_g and ts >= kw_g and tp % CHUNK == 0

    w_main, w_gate = _repack(jnp.swapaxes(w_in, 1, 2), g0=4 * gw, nh=nh_g)
    w_gate = w_gate.astype(BF16)
    w_out_b16 = w_out.astype(BF16)
    lane_pad = lambda a: jnp.pad(a.astype(F32), ((0, 0), (0, GATE_LANES - nh_g)))[:, None, :]
    alog = lane_pad(gdn_a_log)
    dtb = lane_pad(gdn_dt_bias)

    lane_head = jnp.arange(gw) // HEAD_DIM
    tq = _pick(tp, (256, 128, 64, 32, 16, 8))
    tk = _pick(past, (256, 128, 64, 32, 16, 8))
    r = jnp.arange(CHUNK)
    consts = {
        "expand": (jnp.arange(GATE_LANES)[:, None] == lane_head[None, :]).astype(BF16),
        "linc": (r[:, None] >= r[None, :]).astype(BF16),
        "tri": _tri(tq),
    }
    consts_s = dict(consts, tri=_tri(tk), tri_new=_tri(ts))

    s0_p = jnp.zeros((bp, nh_g, HEAD_DIM, HEAD_DIM), F32)
    gconv0_p = jnp.zeros((bp, kw_g - 1, 3 * gw), F32)
    sconv0_p = jnp.zeros((bp, kw_s - 1, cw), F32)

    kpast = jnp.transpose(cache_sb_k, (0, 1, 3, 2, 4)).astype(BF16)
    vpast = jnp.transpose(cache_sb_v, (0, 1, 3, 2, 4)).astype(BF16)

    yp, ys = x_prompt, x_sample
    hp = _norm(x_prompt.reshape(bp * tp, d), norm_pre[0][None, :])
    hs = _norm(x_sample.reshape(bs * ts, d), norm_pre[0][None, :])
    new_p = ([], [], [], [], [])
    new_s = ([], [], [], [], [])
    for l in range(depth):
        lw = {
            "layer": l, "g_post": norm_post[l][None, :],
            "g_pre_next": norm_pre[l + 1][None, :] if l + 1 < depth else None,
            "w_main": w_main, "w_gate": w_gate, "w_out": w_out_b16,
            "gdn_conv_w": gdn_conv_w[l], "alog": alog[l], "dtb": dtb[l], "gn": gdn_norm[l][None, :],
            "sc_conv_w": sc_conv_w[l],
        }
        yp, hp, st_p = _layer(yp, hp, lw, consts, nh_g=nh_g, nh_s=nh_s, cw=cw, sb_past=None,
                              gdn_s0=s0_p, gdn_conv0=gconv0_p, sc_conv0=sconv0_p)
        ys, hs, st_s = _layer(ys, hs, lw, consts_s, nh_g=nh_g, nh_s=nh_s, cw=cw, sb_past=(kpast, vpast, l),
                              gdn_s0=state_gdn[l], gdn_conv0=state_gdn_conv[l], sc_conv0=state_sc_conv[l])
        for i in range(5):
            new_p[i].append(st_p[i])
            new_s[i].append(st_s[i])
    outs_p = [jnp.stack(a, axis=0) for a in new_p]
    outs_s = [jnp.stack(a, axis=0) for a in new_s]
    return (yp, ys, *outs_p, *outs_s)
```

```python
import functools

import jax
import jax.numpy as jnp
from jax import lax
from jax.experimental import pallas as pl
from jax.experimental.pallas import tpu as pltpu

F32 = jnp.float32
BF16 = jnp.bfloat16

HEAD_DIM = 128
CHUNK = 64
EPS = 1e-6
GATE_LANES = 128
VMEM_LIMIT = 56 * 1024 * 1024
OUTPROJ_ROWS = (128, 64, 32, 16)
GDN_HEADS_PER_STEP = (12, 6, 4, 3, 2, 1)
GDN_CHUNKS_PER_STEP = (2, 1)
SB_HEADS_PER_STEP = (6, 4, 3, 2, 1)


def _cparams(sem, vmem=VMEM_LIMIT):
    return pltpu.CompilerParams(dimension_semantics=sem, vmem_limit_bytes=vmem)


def _pick(n, prefs):
    for p in prefs:
        if n % p == 0:
            return p
    return n


def _dot(a, b):
    return jnp.dot(a, b, preferred_element_type=F32)


def _dot_nt(a, b):
    return lax.dot_general(a, b, (((1,), (1,)), ((), ())), preferred_element_type=F32)


def _dot_tn(a, b):
    return lax.dot_general(a, b, (((0,), (0,)), ((), ())), preferred_element_type=F32)


def _split2(x):
    hi = x.astype(BF16)
    lo = (x - hi.astype(F32)).astype(BF16)
    return hi, lo


def _split3(x):
    hi = x.astype(BF16)
    r = x - hi.astype(F32)
    mid = r.astype(BF16)
    lo = (r - mid.astype(F32)).astype(BF16)
    return hi, mid, lo


def _sigmoid(x):
    return 1.0 / (1.0 + jnp.exp(-x))


def _silu(x):
    return x * _sigmoid(x)


def _softplus(x):
    return jnp.maximum(x, 0.0) + jnp.log1p(jnp.exp(-jnp.abs(x)))


def _repack_kernel(w_ref, wm_ref, wg_ref, scr_ref, *, gi, nh, tr):
    i = pl.program_id(1)
    s = 2 * nh

    @pl.when(i < gi)
    def _():
        wm_ref[...] = w_ref[...].astype(BF16)

    @pl.when(i == gi)
    def _():
        wg_ref[...] = jnp.zeros(wg_ref.shape, F32)
        wg_ref[0:nh, :] = w_ref[0:nh, :]
        wg_ref[GATE_LANES:GATE_LANES + nh, :] = w_ref[nh:s, :]

    @pl.when(i > gi)
    def _():
        scr_ref[tr - s:tr, :] = w_ref[0:s, :]
        wm_ref[...] = scr_ref[...].astype(BF16)

    @pl.when((i >= gi) & (i < pl.num_programs(1) - 1))
    def _():
        scr_ref[0:tr - s, :] = w_ref[s:tr, :]


def _repack(w_in_t, *, g0, nh):
    depth, d_in, d = w_in_t.shape
    n = d_in - 2 * nh
    tr = next(t for t in (512, 256, 128, 64, 32, 16, 8) if g0 % t == 0 and n % t == 0 and t > 2 * nh)
    gi = g0 // tr
    kern = functools.partial(_repack_kernel, gi=gi, nh=nh, tr=tr)
    return pl.pallas_call(
        kern,
        grid=(depth, n // tr + 1),
        in_specs=[pl.BlockSpec((None, tr, d), lambda l, i: (l, i, 0))],
        out_specs=[pl.BlockSpec((None, tr, d), lambda l, i: (l, jnp.where(i < gi, i, i - 1), 0)),
                   pl.BlockSpec((None, 2 * GATE_LANES, d), lambda l, i: (l, 0, 0))],
        out_shape=[jax.ShapeDtypeStruct((depth, n, d), BF16),
                   jax.ShapeDtypeStruct((depth, 2 * GATE_LANES, d), F32)],
        scratch_shapes=[pltpu.VMEM((tr, d), F32)],
        compiler_params=_cparams(("parallel", "arbitrary")),
        name="repack",
    )(w_in_t)


def _norm_kernel(x_ref, g_ref, h_ref):
    x = x_ref[...]
    ms = jnp.mean(x * x, axis=-1, keepdims=True)
    h_ref[...] = ((x * lax.rsqrt(ms + EPS)) * g_ref[...]).astype(h_ref.dtype)


def _norm(x2d, g_row):
    m, d = x2d.shape
    tm = _pick(m, (256, 128, 64, 32, 16, 8))
    return pl.pallas_call(
        _norm_kernel,
        grid=(m // tm,),
        in_specs=[pl.BlockSpec((tm, d), lambda i: (i, 0)), pl.BlockSpec((1, d), lambda i: (0, 0))],
        out_specs=pl.BlockSpec((tm, d), lambda i: (i, 0)),
        out_shape=jax.ShapeDtypeStruct((m, d), BF16),
        compiler_params=_cparams(("parallel",)),
        name="norm",
    )(x2d, g_row)


def _proj_kernel(h_ref, w_ref, wg_ref, o_ref, og_ref):
    @pl.when(pl.program_id(1) == 0)
    def _():
        og_ref[...] = _dot_nt(h_ref[...], wg_ref[...])

    o_ref[...] = _dot_nt(h_ref[...], w_ref[...])


def _proj(h2d, w_main, w_gate, layer):
    m, d = h2d.shape
    n = w_main.shape[1]
    tm = _pick(m, (1024, 512, 256, 128, 64, 32, 16))
    tn = _pick(n, (1024, 512, 256, 128))
    return pl.pallas_call(
        _proj_kernel,
        grid=(m // tm, n // tn),
        in_specs=[
            pl.BlockSpec((tm, d), lambda i, j: (i, 0)),
            pl.BlockSpec((None, tn, d), lambda i, j: (layer, j, 0)),
            pl.BlockSpec((None, 2 * GATE_LANES, d), lambda i, j: (layer, 0, 0)),
        ],
        out_specs=[
            pl.BlockSpec((tm, tn), lambda i, j: (i, j)),
            pl.BlockSpec((tm, 2 * GATE_LANES), lambda i, j: (i, 0)),
        ],
        out_shape=[jax.ShapeDtypeStruct((m, n), F32),
                   jax.ShapeDtypeStruct((m, 2 * GATE_LANES), F32)],
        compiler_params=_cparams(("parallel", "arbitrary")),
        name="proj",
    )(h2d, w_main, w_gate)


def _gdn_pre_kernel(x_ref, og_ref, cw_ref, pre_ref, alog_ref, dtb_ref, e_ref, linc_ref,
                    qkv_ref, bx_ref, gx_ref, xs_ref, *, tt, tt_out, nh, kw):
    i = pl.program_id(1)
    hist = kw - 1
    base = 8

    @pl.when(i == 0)
    def _():
        xs_ref[base - hist:base, :] = pre_ref[0]

    xs_ref[base:base + tt, :] = x_ref[...]
    for c in range(3 * nh):
        cs = slice(c * HEAD_DIM, (c + 1) * HEAD_DIM)
        y = xs_ref[base - hist:base - hist + tt, cs] * cw_ref[0:1, cs]
        for t in range(1, kw):
            y = y + xs_ref[base - hist + t:base - hist + t + tt, cs] * cw_ref[t:t + 1, cs]
        y = _silu(y)
        if c < 2 * nh:
            y = y * lax.rsqrt(jnp.sum(y * y, axis=-1, keepdims=True) + EPS)
        if c < nh:
            y = y * (HEAD_DIM ** -0.5)
        qkv_ref[0:tt, cs] = y.astype(qkv_ref.dtype)
    if tt_out > tt:
        qkv_ref[tt:tt_out, :] = jnp.zeros((tt_out - tt, qkv_ref.shape[1]), qkv_ref.dtype)
    xs_ref[base - hist:base, :] = xs_ref[base + tt - hist:base + tt, :]

    og = og_ref[...]
    g = -jnp.exp(alog_ref[...]) * _softplus(og[:, :GATE_LANES] + dtb_ref[...])
    beta = _sigmoid(og[:, GATE_LANES:])
    if tt_out > tt:
        pad = jnp.zeros((tt_out - tt, GATE_LANES), F32)
        g = jnp.concatenate([g, pad], axis=0)
        beta = jnp.concatenate([beta, pad], axis=0)
    e = e_ref[...]
    linc = linc_ref[...]
    bx = None
    for p in _split2(beta):
        t = _dot(p, e)
        bx = t if bx is None else bx + t
    bx_ref[...] = bx
    for r in range(tt_out // CHUNK):
        rs = slice(r * CHUNK, (r + 1) * CHUNK)
        gc = None
        for p in _split3(g[rs]):
            t = _dot(linc, p)
            gc = t if gc is None else gc + t
        gcx = None
        for p in _split2(gc):
            t = _dot(p, e)
            gcx = t if gcx is None else gcx + t
        gx_ref[rs, :] = gcx


def _gdn_pre(proj, og, conv_w, prefix, alog_row, dtb_row, expand, linc, *, b, t, t_out, nh):
    gw = nh * HEAD_DIM
    kw = conv_w.shape[0]
    if t_out > t:
        tt, tt_out = t, t_out
    else:
        tt = _pick(t, (256, 128, 64))
        tt_out = tt
    nt = t // tt
    kern = functools.partial(_gdn_pre_kernel, tt=tt, tt_out=tt_out, nh=nh, kw=kw)
    return pl.pallas_call(
        kern,
        grid=(b, nt),
        in_specs=[
            pl.BlockSpec((tt, 3 * gw), lambda bi, i: (bi * nt + i, 0)),
            pl.BlockSpec((tt, 2 * GATE_LANES), lambda bi, i: (bi * nt + i, 0)),
            pl.BlockSpec((kw, 3 * gw), lambda bi, i: (0, 0)),
            pl.BlockSpec((1, kw - 1, 3 * gw), lambda bi, i: (bi, 0, 0)),
            pl.BlockSpec((1, GATE_LANES), lambda bi, i: (0, 0)),
            pl.BlockSpec((1, GATE_LANES), lambda bi, i: (0, 0)),
            pl.BlockSpec((GATE_LANES, gw), lambda bi, i: (0, 0)),
            pl.BlockSpec((CHUNK, CHUNK), lambda bi, i: (0, 0)),
        ],
        out_specs=[
            pl.BlockSpec((tt_out, 3 * gw), lambda bi, i: (bi * nt + i, 0)),
            pl.BlockSpec((tt_out, gw), lambda bi, i: (bi * nt + i, 0)),
            pl.BlockSpec((tt_out, gw), lambda bi, i: (bi * nt + i, 0)),
        ],
        out_shape=[jax.ShapeDtypeStruct((b * nt * tt_out, 3 * gw), F32),
                   jax.ShapeDtypeStruct((b * nt * tt_out, gw), F32),
                   jax.ShapeDtypeStruct((b * nt * tt_out, gw), F32)],
        scratch_shapes=[pltpu.VMEM((tt + 8, 3 * gw), F32)],
        compiler_params=_cparams(("parallel", "arbitrary")),
        name="gdn_pre",
    )(proj, og, conv_w, prefix, alog_row, dtb_row, expand, linc)


def _gdn_kernel(q_ref, k_ref, v_ref, bx_ref, gx_ref, az_ref, s0_ref, gn_ref,
                o_ref, sout_ref, s_ref, *, hg, cpb, t_valid):
    i = pl.program_id(2)
    c2 = 2 * CHUNK

    @pl.when(i == 0)
    def _():
        s_ref[...] = s0_ref[0]

    row = lax.broadcasted_iota(jnp.int32, (CHUNK, c2), 0)
    lane = lax.broadcasted_iota(jnp.int32, (CHUNK, c2), 1)
    col = jnp.where(lane >= CHUNK, lane - CHUNK, lane)
    incl = row >= col
    strict = row > col
    right = lane >= CHUNK
    eye_right = jnp.where(right & (row == col), 1.0, 0.0).astype(F32)
    onehot0 = jnp.where(lane == 0, 1.0, 0.0).astype(BF16)
    zrow_b = jnp.zeros((CHUNK, HEAD_DIM), BF16)

    units = [(c, h) for c in range(cpb) for h in range(hg)]
    rsl = lambda c: slice(c * CHUNK, (c + 1) * CHUNK)
    csl = lambda h: slice(h * HEAD_DIM, (h + 1) * HEAD_DIM)
    st = {}
    for u in units:
        c, h = u
        q = q_ref[rsl(c), csl(h)]
        k = k_ref[rsl(c), csl(h)]
        v = v_ref[rsl(c), csl(h)]
        bx = bx_ref[rsl(c), csl(h)]
        gx = gx_ref[rsl(c), csl(h)]
        ex = jnp.exp(gx)
        gl = gx[CHUNK - 1:CHUNK, :]
        kb = k * bx
        st[u] = dict(gx=gx, k=k, kb=kb, q=q, egl=jnp.exp(gl), kd=(k * jnp.exp(gl - gx)).astype(BF16),
                     qd=q * ex, rhs=jnp.concatenate([v * bx, kb * ex], axis=1).astype(BF16))

    for u in units:
        rt = None
        for p in _split2(st[u]["gx"]):
            t = _dot_nt(onehot0, jnp.concatenate([p, p], axis=0))
            rt = t if rt is None else rt + t
        st[u]["rt"] = rt
    for u in units:
        s = st[u]
        kbf = s["k"].astype(BF16)
        s["qk"] = _dot_nt(jnp.concatenate([s["kb"], s["q"]], axis=0).astype(BF16),
                          jnp.concatenate([kbf, kbf], axis=0))
    for u in units:
        s = st[u]
        dec = jnp.where(incl, jnp.exp(jnp.where(incl, s["gx"] - s["rt"], 0.0)), 0.0)
        lmat = jnp.where(strict, s["qk"][:CHUNK] * dec, 0.0)
        s["attn"] = jnp.where(incl & (lane < CHUNK), s["qk"][CHUNK:] * dec, 0.0).astype(BF16)
        s["w"] = jnp.where(right, eye_right, -lmat)
    for _ in range(6):
        for u in units:
            s = st[u]
            wb = s["w"].astype(BF16)
            s["prod"] = _dot(wb, jnp.concatenate([wb, zrow_b], axis=0))
        for u in units:
            s = st[u]
            s["w"] = s["prod"] + jnp.where(right, s["w"], 0.0)
    for u in units:
        s = st[u]
        rhs = jnp.concatenate([jnp.zeros_like(s["rhs"]), s["rhs"]], axis=0)
        s["tu"] = _dot(s["w"].astype(BF16), rhs)

    state = [s_ref[h] for h in range(hg)]
    for c in range(cpb):
        nvalid = min(max(t_valid - c * CHUNK, 0), CHUNK)
        ws = []
        for h in range(hg):
            s = st[(c, h)]
            lhs = jnp.concatenate([s["tu"][:, HEAD_DIM:], s["qd"]], axis=0).astype(BF16)
            ws.append(_dot(lhs, state[h].astype(BF16)))
        vnb = [(st[(c, h)]["tu"][:, :HEAD_DIM] - ws[h][:CHUNK]).astype(BF16) for h in range(hg)]
        outs = []
        for h in range(hg):
            s = st[(c, h)]
            outs.append(ws[h][CHUNK:] + _dot(s["attn"], jnp.concatenate([vnb[h], zrow_b], axis=0)))
            state[h] = state[h] * s["egl"] + _dot_tn(s["kd"], vnb[h])
        if nvalid > 0:
            for h in range(hg):
                ov = outs[h][:nvalid]
                on = (ov * lax.rsqrt(jnp.mean(ov * ov, axis=-1, keepdims=True) + EPS)) * gn_ref[...]
                orow = slice(c * CHUNK, c * CHUNK + nvalid)
                o_ref[orow, csl(h)] = (on * _silu(az_ref[orow, csl(h)])).astype(o_ref.dtype)
    for h in range(hg):
        s_ref[h] = state[h]

    @pl.when(i == pl.num_programs(2) - 1)
    def _():
        sout_ref[0] = s_ref[...]


def _gdn(qkv, bx, gx, proj, s0, gn_row, *, b, t_pad, t_valid, nh, az_col0):
    gw = nh * HEAD_DIM
    hg = _pick(nh, GDN_HEADS_PER_STEP)
    ng = nh // hg
    if t_valid < t_pad:
        cpb = t_pad // CHUNK
    else:
        cpb = _pick(t_pad // CHUNK, GDN_CHUNKS_PER_STEP)
    rb = cpb * CHUNK
    nblk = t_pad // rb
    rv = min(rb, t_valid)
    assert nblk == 1 or rv == rb
    hw = hg * HEAD_DIM
    azb = az_col0 // hw
    assert az_col0 % hw == 0
    kern = functools.partial(_gdn_kernel, hg=hg, cpb=cpb, t_valid=rv)
    qspec = lambda off: pl.BlockSpec((rb, hw), lambda bi, g, i, off=off: (bi * nblk + i, off + g))
    return pl.pallas_call(
        kern,
        grid=(b, ng, nblk),
        in_specs=[
            qspec(0), qspec(ng), qspec(2 * ng),
            pl.BlockSpec((rb, hw), lambda bi, g, i: (bi * nblk + i, g)),
            pl.BlockSpec((rb, hw), lambda bi, g, i: (bi * nblk + i, g)),
            pl.BlockSpec((rv, hw), lambda bi, g, i: (bi * nblk + i, azb + g)),
            pl.BlockSpec((1, hg, HEAD_DIM, HEAD_DIM), lambda bi, g, i: (bi, g, 0, 0)),
            pl.BlockSpec((1, HEAD_DIM), lambda bi, g, i: (0, 0)),
        ],
        out_specs=[
            pl.BlockSpec((rv, hw), lambda bi, g, i: (bi * nblk + i, g)),
            pl.BlockSpec((1, hg, HEAD_DIM, HEAD_DIM), lambda bi, g, i: (bi, g, 0, 0)),
        ],
        out_shape=[jax.ShapeDtypeStruct((b * nblk * rv, gw), BF16),
                   jax.ShapeDtypeStruct((b, nh, HEAD_DIM, HEAD_DIM), F32)],
        scratch_shapes=[pltpu.VMEM((hg, HEAD_DIM, HEAD_DIM), F32)],
        compiler_params=_cparams(("parallel", "parallel", "arbitrary")),
        name="gdn",
    )(qkv, qkv, qkv, bx, gx, proj, s0, gn_row)


SB_Z2_SCALE = (HEAD_DIM ** -0.5) * 1.4426950408889634

def _neg_abs(x):
    bits = lax.bitcast_convert_type(x, jnp.uint32) | jnp.uint32(0x80000000)
    return lax.bitcast_convert_type(bits, F32)


def _sb_blocks(q_bfs, k_blks, v_blks, tri, mask):
    z2s = [_dot_nt(q, k.astype(BF16)) for q, k in zip(q_bfs, k_blks)]
    lss, nlks = [], []
    for z2 in z2s:
        ls = jnp.minimum(z2, 0.0) - jnp.log2(1.0 + jnp.exp2(_neg_abs(z2)))
        lss.append(ls)
        nlk = z2 - ls
        if mask is not None:
            nlk = jnp.where(mask, nlk, 0.0)
        nlks.append(nlk)
    betw = [_dot(nlk.astype(BF16), tri) for nlk in nlks]
    pvs = []
    for ls, bt, v in zip(lss, betw, v_blks):
        a = jnp.exp2(ls - bt)
        if mask is not None:
            a = jnp.where(mask, a, 0.0)
        pvs.append(_dot(a.astype(BF16), v.astype(BF16)))
    return pvs, [jnp.sum(nlk, axis=-1, keepdims=True) for nlk in nlks]


def _sb_sweep(q_ref, z_ref, o_ref, acc_ref, drop_ref, first, blocks, nblocks, tri, *, hb):
    hsl = lambda h: slice(h * HEAD_DIM, (h + 1) * HEAD_DIM)
    q_bfs = [(q_ref[:, hsl(h)] * SB_Z2_SCALE).astype(BF16) for h in range(hb)]
    rows = q_ref.shape[0]
    k0, v0, tri0, mask0 = first
    pvs, sums = _sb_blocks(q_bfs, k0, v0, tri0, mask0)
    for h in range(hb):
        acc_ref[:, hsl(h)] = pvs[h]
        drop_ref[:, hsl(h)] = jnp.broadcast_to(sums[h], (rows, HEAD_DIM))

    def body(step, carry):
        ks, vs = blocks(step)
        pvs, sums = _sb_blocks(q_bfs, ks, vs, tri, None)
        for h in range(hb):
            drop = drop_ref[:, hsl(h)]
            acc_ref[:, hsl(h)] += pvs[h] * jnp.exp2(-drop)
            drop_ref[:, hsl(h)] = drop + jnp.broadcast_to(sums[h], (rows, HEAD_DIM))
        return carry

    lax.fori_loop(0, nblocks, body, 0)
    o_ref[...] = (acc_ref[...] * _silu(z_ref[...])).astype(o_ref.dtype)


def _sb_prompt_kernel(q_ref, k_ref, v_ref, z_ref, tri_ref, o_ref, acc_ref, drop_ref, *, tq, hb):
    qi = pl.program_id(2)
    hsl = lambda h: slice(h * HEAD_DIM, (h + 1) * HEAD_DIM)
    tri = tri_ref[...]
    row = lax.broadcasted_iota(jnp.int32, (tq, tq), 0)
    col = lax.broadcasted_iota(jnp.int32, (tq, tq), 1)

    def load(r0):
        return ([k_ref[pl.ds(r0, tq), hsl(h)] for h in range(hb)],
                [v_ref[pl.ds(r0, tq), hsl(h)] for h in range(hb)])

    k0, v0 = load(pl.multiple_of(qi * tq, tq))
    blocks = lambda step: load(pl.multiple_of((qi - 1 - step) * tq, tq))
    _sb_sweep(q_ref, z_ref, o_ref, acc_ref, drop_ref, (k0, v0, tri, col < row), blocks, qi, tri, hb=hb)


def _sb_heads_per_step(nh, q_col0):
    for hb in SB_HEADS_PER_STEP:
        if nh % hb == 0 and (q_col0 // HEAD_DIM) % hb == 0:
            return hb
    return 1


def _sb_prompt(proj, tri, *, b, t, nh, q_col0):
    sw = nh * HEAD_DIM
    tq = tri.shape[0]
    nq = t // tq
    hb = _sb_heads_per_step(nh, q_col0)
    hw = hb * HEAD_DIM
    ng = nh // hb
    qb = q_col0 // hw
    kern = functools.partial(_sb_prompt_kernel, tq=tq, hb=hb)

    def rows(off):
        return pl.BlockSpec((tq, hw), lambda bi, g, i: (bi * nq + i, qb + off * ng + g))

    def seq(off):
        return pl.BlockSpec((t, hw), lambda bi, g, i: (bi, qb + off * ng + g), pipeline_mode=pl.Buffered(1))

    return pl.pallas_call(
        kern,
        grid=(b, ng, nq),
        in_specs=[rows(0), seq(1), seq(2), rows(3), pl.BlockSpec((tq, tq), lambda bi, g, i: (0, 0))],
        out_specs=pl.BlockSpec((tq, hw), lambda bi, g, i: (bi * nq + i, g)),
        out_shape=jax.ShapeDtypeStruct((b * t, sw), BF16),
        scratch_shapes=[pltpu.VMEM((tq, hw), F32), pltpu.VMEM((tq, hw), F32)],
        compiler_params=_cparams(("parallel", "parallel", "arbitrary")),
        name="sb_prompt",
    )(proj, proj, proj, proj, tri)


def _sb_sample_kernel(q_ref, k_ref, v_ref, z_ref, kp_ref, vp_ref, trin_ref, tri_ref, o_ref,
                      acc_ref, drop_ref, *, t, tk, npast, hb):
    hsl = lambda h: slice(h * HEAD_DIM, (h + 1) * HEAD_DIM)
    row = lax.broadcasted_iota(jnp.int32, (t, t), 0)
    col = lax.broadcasted_iota(jnp.int32, (t, t), 1)
    k0 = [k_ref[:, hsl(h)] for h in range(hb)]
    v0 = [v_ref[:, hsl(h)] for h in range(hb)]

    def blocks(step):
        r0 = pl.multiple_of((npast - 1 - step) * tk, tk)
        return ([kp_ref[h, pl.ds(r0, tk), :] for h in range(hb)],
                [vp_ref[h, pl.ds(r0, tk), :] for h in range(hb)])

    _sb_sweep(q_ref, z_ref, o_ref, acc_ref, drop_ref, (k0, v0, trin_ref[...], col < row), blocks, npast,
              tri_ref[...], hb=hb)


def _sb_sample(proj, k_past, v_past, layer, tri_new, tri, *, b, t, nh, q_col0):
    sw = nh * HEAD_DIM
    past = k_past.shape[3]
    tk = tri.shape[0]
    assert past % tk == 0
    hb = _sb_heads_per_step(nh, q_col0)
    hw = hb * HEAD_DIM
    ng = nh // hb
    qb = q_col0 // hw
    kern = functools.partial(_sb_sample_kernel, t=t, tk=tk, npast=past // tk, hb=hb)

    def cur(off):
        return pl.BlockSpec((t, hw), lambda bi, g: (bi, qb + off * ng + g))

    old = pl.BlockSpec((None, None, hb, past, HEAD_DIM), lambda bi, g: (layer, bi, g, 0, 0))
    return pl.pallas_call(
        kern,
        grid=(b, ng),
        in_specs=[
            cur(0), cur(1), cur(2), cur(3), old, old,
            pl.BlockSpec((t, t), lambda bi, g: (0, 0)),
            pl.BlockSpec((tk, tk), lambda bi, g: (0, 0)),
        ],
        out_specs=pl.BlockSpec((t, hw), lambda bi, g: (bi, g)),
        out_shape=jax.ShapeDtypeStruct((b * t, sw), BF16),
        scratch_shapes=[pltpu.VMEM((t, hw), F32), pltpu.VMEM((t, hw), F32)],
        compiler_params=_cparams(("parallel", "parallel")),
        name="sb_sample",
    )(proj, proj, proj, proj, k_past, v_past, tri_new, tri)


def _sc_kernel(cb_ref, cc_ref, ch_ref, cz_ref, cw_ref, pre_ref, o_ref, tail_ref, us_ref, *, tt, kw):
    i = pl.program_id(1)
    hist = kw - 1
    base = 8

    @pl.when(i == 0)
    def _():
        us_ref[base - hist:base, :] = pre_ref[0]

    us_ref[base:base + tt, :] = cc_ref[...] * ch_ref[...]
    y = us_ref[base - hist:base - hist + tt, :] * cw_ref[0:1, :]
    for t in range(1, kw):
        y = y + us_ref[base - hist + t:base - hist + t + tt, :] * cw_ref[t:t + 1, :]
    o_ref[...] = ((cb_ref[...] * y) * _silu(cz_ref[...])).astype(o_ref.dtype)
    tail = us_ref[base + tt - hist:base + tt, :]
    us_ref[base - hist:base, :] = tail

    @pl.when(i == pl.num_programs(1) - 1)
    def _():
        tail_ref[0] = tail


def _sc(proj, conv_w, prefix, *, b, t, cw, c_col0):
    kw = conv_w.shape[0]
    tt = _pick(t, (256, 128, 64, 32, 16, 8))
    nt = t // tt
    cb = c_col0 // cw
    assert c_col0 % cw == 0
    kern = functools.partial(_sc_kernel, tt=tt, kw=kw)
    col = lambda off: pl.BlockSpec((tt, cw), lambda bi, i, off=off: (bi * nt + i, cb + off))
    return pl.pallas_call(
        kern,
        grid=(b, nt),
        in_specs=[
            col(0), col(1), col(2), col(3),
            pl.BlockSpec((kw, cw), lambda bi, i: (0, 0)),
            pl.BlockSpec((1, kw - 1, cw), lambda bi, i: (bi, 0, 0)),
        ],
        out_specs=[
            pl.BlockSpec((tt, cw), lambda bi, i: (bi * nt + i, 0)),
            pl.BlockSpec((1, kw - 1, cw), lambda bi, i: (bi, 0, 0)),
        ],
        out_shape=[jax.ShapeDtypeStruct((b * t, cw), BF16),
                   jax.ShapeDtypeStruct((b, kw - 1, cw), F32)],
        scratch_shapes=[pltpu.VMEM((tt + 8, cw), F32)],
        compiler_params=_cparams(("parallel", "arbitrary")),
        name="sc",
    )(proj, proj, proj, proj, conv_w, prefix)


def _outproj_kernel(a_ref, b_ref, c_ref, wa_ref, wb_ref, wc_ref, x_ref, g_ref, gn_ref, o_ref, *h_ref):
    y = _dot(a_ref[...], wa_ref[...]) + _dot(b_ref[...], wb_ref[...]) + _dot(c_ref[...], wc_ref[...])
    ms = jnp.mean(y * y, axis=-1, keepdims=True)
    xn = x_ref[...] + (y * lax.rsqrt(ms + EPS)) * g_ref[...]
    o_ref[...] = xn
    if h_ref:
        msn = jnp.mean(xn * xn, axis=-1, keepdims=True)
        h_ref[0][...] = ((xn * lax.rsqrt(msn + EPS)) * gn_ref[...]).astype(BF16)


def _outproj(oa, ob, oc, w_out, layer, x2d, g_row, g_next):
    m = oa.shape[0]
    d = w_out.shape[2]
    ka, kb, kc = oa.shape[1], ob.shape[1], oc.shape[1]
    assert ka % kb == 0 and (ka + kb) % kc == 0
    tm = _pick(m, OUTPROJ_ROWS)
    emit_h = g_next is not None

    def wspec(rows, blk):
        return pl.BlockSpec((None, rows, d), lambda i: (layer, blk, 0), pipeline_mode=pl.Buffered(1))

    row = pl.BlockSpec((tm, d), lambda i: (i, 0))
    gain = pl.BlockSpec((1, d), lambda i: (0, 0))
    outs = pl.pallas_call(
        _outproj_kernel,
        grid=(m // tm,),
        in_specs=[
            pl.BlockSpec((tm, ka), lambda i: (i, 0)),
            pl.BlockSpec((tm, kb), lambda i: (i, 0)),
            pl.BlockSpec((tm, kc), lambda i: (i, 0)),
            wspec(ka, 0), wspec(kb, ka // kb), wspec(kc, (ka + kb) // kc),
            row, gain, gain,
        ],
        out_specs=[row, row] if emit_h else [row],
        out_shape=([jax.ShapeDtypeStruct((m, d), F32), jax.ShapeDtypeStruct((m, d), BF16)] if emit_h
                   else [jax.ShapeDtypeStruct((m, d), F32)]),
        compiler_params=_cparams(("parallel",)),
        name="outproj",
    )(oa, ob, oc, w_out, w_out, w_out, x2d, g_row, g_next if emit_h else g_row)
    return (outs[0], outs[1]) if emit_h else (outs[0], None)


def _layer(x, h2d, lw, consts, *, nh_g, nh_s, cw, sb_past, gdn_s0, gdn_conv0, sc_conv0):
    b, t, d = x.shape
    gw, sw = nh_g * HEAD_DIM, nh_s * HEAD_DIM
    x2d = x.reshape(b * t, d)
    proj, og = _proj(h2d, lw["w_main"], lw["w_gate"], lw["layer"])

    t_pad = -(-t // CHUNK) * CHUNK
    qkv, bx, gx = _gdn_pre(proj, og, lw["gdn_conv_w"], gdn_conv0, lw["alog"], lw["dtb"],
                           consts["expand"], consts["linc"], b=b, t=t, t_out=t_pad, nh=nh_g)
    o_a, s_new = _gdn(qkv, bx, gx, proj, gdn_s0, lw["gn"], b=b, t_pad=t_pad, t_valid=t,
                      nh=nh_g, az_col0=3 * gw)

    q_col0 = 4 * gw
    if sb_past is None:
        o_b = _sb_prompt(proj, consts["tri"], b=b, t=t, nh=nh_s, q_col0=q_col0)
    else:
        o_b = _sb_sample(proj, sb_past[0], sb_past[1], sb_past[2], consts["tri_new"], consts["tri"],
                         b=b, t=t, nh=nh_s, q_col0=q_col0)

    c_col0 = 4 * gw + 4 * sw
    o_c, sc_tail = _sc(proj, lw["sc_conv_w"], sc_conv0, b=b, t=t, cw=cw, c_col0=c_col0)

    x_new, h_next = _outproj(o_a, o_b, o_c, lw["w_out"], lw["layer"], x2d, lw["g_post"], lw["g_pre_next"])
    x_new = x_new.reshape(b, t, d)

    kcol = q_col0 + sw
    proj3 = proj.reshape(b, t, proj.shape[1])
    kb = proj3[:, :, kcol:kcol + sw].reshape(b, t, nh_s, HEAD_DIM)
    vb = proj3[:, :, kcol + sw:kcol + 2 * sw].reshape(b, t, nh_s, HEAD_DIM)
    kw = lw["gdn_conv_w"].shape[0]
    conv_new = proj3[:, t - (kw - 1):, :3 * gw]
    return x_new, h_next, (kb, vb, s_new, conv_new, sc_tail)


def _tri(n):
    r = jnp.arange(n)
    return (r[:, None] > r[None, :]).astype(BF16)


def kernel(x_prompt, x_sample, cache_sb_k, cache_sb_v, state_gdn, state_gdn_conv, state_sc_conv,
           w_in, w_out, norm_pre, norm_post, gdn_conv_w, gdn_a_log, gdn_dt_bias, gdn_norm, sc_conv_w):
    depth, d, _ = w_in.shape
    nh_g = gdn_a_log.shape[1]
    nh_s = cache_sb_k.shape[3]
    cw = sc_conv_w.shape[2]
    gw, sw = nh_g * HEAD_DIM, nh_s * HEAD_DIM
    bp, tp, _ = x_prompt.shape
    bs, ts, _ = x_sample.shape
    past = cache_sb_k.shape[2]
    kw_g = gdn_conv_w.shape[1]
    kw_s = sc_conv_w.shape[1]
    assert nh_g <= GATE_LANES and tp >= kw_g and ts >= kw_g and tp % CHUNK == 0

    w_main, w_gate = _repack(jnp.swapaxes(w_in, 1, 2), g0=4 * gw, nh=nh_g)
    w_gate = w_gate.astype(BF16)
    w_out_b16 = w_out.astype(BF16)
    lane_pad = lambda a: jnp.pad(a.astype(F32), ((0, 0), (0, GATE_LANES - nh_g)))[:, None, :]
    alog = lane_pad(gdn_a_log)
    dtb = lane_pad(gdn_dt_bias)

    lane_head = jnp.arange(gw) // HEAD_DIM
    tq = _pick(tp, (256, 128, 64, 32, 16, 8))
    tk = _pick(past, (256, 128, 64, 32, 16, 8))
    r = jnp.arange(CHUNK)
    consts = {
        "expand": (jnp.arange(GATE_LANES)[:, None] == lane_head[None, :]).astype(BF16),
        "linc": (r[:, None] >= r[None, :]).astype(BF16),
        "tri": _tri(tq),
    }
    consts_s = dict(consts, tri=_tri(tk), tri_new=_tri(ts))

    s0_p = jnp.zeros((bp, nh_g, HEAD_DIM, HEAD_DIM), F32)
    gconv0_p = jnp.zeros((bp, kw_g - 1, 3 * gw), F32)
    sconv0_p = jnp.zeros((bp, kw_s - 1, cw), F32)

    kpast = jnp.transpose(cache_sb_k, (0, 1, 3, 2, 4)).astype(BF16)
    vpast = jnp.transpose(cache_sb_v, (0, 1, 3, 2, 4)).astype(BF16)

    yp, ys = x_prompt, x_sample
    hp = _norm(x_prompt.reshape(bp * tp, d), norm_pre[0][None, :])
    hs = _norm(x_sample.reshape(bs * ts, d), norm_pre[0][None, :])
    new_p = ([], [], [], [], [])
    new_s = ([], [], [], [], [])
    for l in range(depth):
        lw = {
            "layer": l, "g_post": norm_post[l][None, :],
            "g_pre_next": norm_pre[l + 1][None, :] if l + 1 < depth else None,
            "w_main": w_main, "w_gate": w_gate, "w_out": w_out_b16,
            "gdn_conv_w": gdn_conv_w[l], "alog": alog[l], "dtb": dtb[l], "gn": gdn_norm[l][None, :],
            "sc_conv_w": sc_conv_w[l],
        }
        yp, hp, st_p = _layer(yp, hp, lw, consts, nh_g=nh_g, nh_s=nh_s, cw=cw, sb_past=None,
                              gdn_s0=s0_p, gdn_conv0=gconv0_p, sc_conv0=sconv0_p)
        ys, hs, st_s = _layer(ys, hs, lw, consts_s, nh_g=nh_g, nh_s=nh_s, cw=cw, sb_past=(kpast, vpast, l),
                              gdn_s0=state_gdn[l], gdn_conv0=state_gdn_conv[l], sc_conv0=state_sc_conv[l])
        for i in range(5):
            new_p[i].append(st_p[i])
            new_s[i].append(st_s[i])
    outs_p = [jnp.stack(a, axis=0) for a in new_p]
    outs_s = [jnp.stack(a, axis=0) for a in new_s]
    return (yp, ys, *outs_p, *outs_s)
```
